```python
import math, functools
import jax, jax.numpy as jnp
from jax import lax
import numpy as np

D_MODEL = 1024
BATCH = 4
SEQ = 4096
DEPTH = 1
DEC_BATCH = 128
DEC_SEQ = 8
PAST_LEN = 2048
PAGE_SIZE = 128

DA_HEADS = 4
DA_HD = 64
DA_VD = 2 * DA_HD
DA_WIDTH = DA_HEADS * DA_VD
DA_QK = DA_HEADS * 2 * DA_HD
RW_HEADS = 8
RW_HD = 64
RW_WIDTH = RW_HEADS * RW_HD
RW_W_RANK = 64
RW_A_RANK = 64
RW_G_RANK = 128
RW_PROJ = 3 * RW_WIDTH + RW_W_RANK + RW_A_RANK + RW_G_RANK
RW_GN_EPS = 64e-5
MIX_WIDTH = DA_WIDTH + RW_WIDTH
IN_PROJ = 2 * DA_QK + DA_WIDTH + RW_PROJ
PEER_HEADS = 8
PEER_NKEYS = 128
PEER_EXPERTS = PEER_NKEYS * PEER_NKEYS
PEER_KD = 128
PEER_TOPK = 16
PEER_CHUNK = 128
ROPE_THETA = 10000.0
Q_BLOCK = 128
NORM_EPS = 1e-6
NEG_INF = -1e30

kernel_name = 'hybrid_diffattn_rwkv7_peer_step'


def rms_norm(x, g):
    xf = x.astype(jnp.float32)
    y = xf * lax.rsqrt(jnp.mean(xf * xf, axis=-1, keepdims=True) + NORM_EPS)
    return (y * g.astype(jnp.float32)).astype(x.dtype)


def rope(x, pos):
    half = DA_HD // 2
    inv = ROPE_THETA ** (-jnp.arange(half, dtype=jnp.float32) / half)
    ang = pos.astype(jnp.float32)[:, None] * inv[None, :]
    cos = jnp.cos(ang)[None, :, None, None, :]
    sin = jnp.sin(ang)[None, :, None, None, :]
    xf = x.astype(jnp.float32)
    x1, x2 = xf[..., :half], xf[..., half:]
    return jnp.concatenate([x1 * cos - x2 * sin, x1 * sin + x2 * cos], axis=-1).astype(x.dtype)


def diff_attend(q, k, v, q_pos, k_pos, lam):
    s = jnp.einsum('bqhcd,bkhcd->bhcqk', q.astype(jnp.float32), k.astype(jnp.float32)) * (DA_HD ** -0.5)
    mask = k_pos[None, :] <= q_pos[:, None]
    p = jax.nn.softmax(jnp.where(mask, s, NEG_INF), axis=-1)
    a = p[:, :, 0] - lam * p[:, :, 1]
    return jnp.einsum('bhqk,bkhe->bqhe', a, v.astype(jnp.float32))


def attend_prompt(q, k, v, lam, pos):
    B, T = q.shape[0], q.shape[1]
    nb = T // Q_BLOCK
    qb = jnp.moveaxis(q.reshape(B, nb, Q_BLOCK, DA_HEADS, 2, DA_HD), 1, 0)
    pb = pos.reshape(nb, Q_BLOCK)
    ob = lax.map(lambda a: diff_attend(a[0], k, v, a[1], pos, lam), (qb, pb))
    return jnp.moveaxis(ob, 0, 1).reshape(B, T, DA_HEADS, DA_VD)


def attend_sample(q, k, v, lam, pos, k_cache, v_cache, page_table):
    Bd, T = q.shape[0], q.shape[1]
    past = page_table.shape[1] * PAGE_SIZE
    kp = k_cache[page_table].reshape(Bd, past, DA_HEADS, 2, DA_HD)
    vp = v_cache[page_table].reshape(Bd, past, DA_HEADS, DA_VD)
    k_all = jnp.concatenate([kp, k.astype(kp.dtype)], axis=1)
    v_all = jnp.concatenate([vp, v.astype(vp.dtype)], axis=1)
    k_pos = jnp.arange(past + T, dtype=jnp.int32)
    return diff_attend(q, k_all, v_all, pos, k_pos, lam)


def rwkv_time_mix(p, shift0, S0, mu, w0, w_up, a0, a_up, g_up, k_k, k_a, r_k, ln_w, ln_b):
    B, T, _ = p.shape
    W = RW_WIDTH
    p_prev = jnp.concatenate([shift0[:, None].astype(p.dtype), p[:, :-1]], axis=1)
    ps = p + (p_prev - p) * mu
    r, k, v = ps[..., :W], ps[..., W:2 * W], ps[..., 2 * W:3 * W]
    o = 3 * W
    xw = ps[..., o:o + RW_W_RANK]
    o += RW_W_RANK
    xa = ps[..., o:o + RW_A_RANK]
    o += RW_A_RANK
    xg = ps[..., o:o + RW_G_RANK]
    w = -jax.nn.softplus(-(w0 + jnp.tanh(xw) @ w_up)) - 0.5
    decay = jnp.exp(-jnp.exp(w.astype(jnp.float32)))
    a = jax.nn.sigmoid(a0 + xa @ a_up)
    g = jax.nn.sigmoid(xg) @ g_up
    heads = lambda t: t.reshape(B, T, RW_HEADS, RW_HD).astype(jnp.float32)
    r, k, v, a, decay = heads(r), heads(k), heads(v), heads(a), heads(decay)
    kk = k * k_k.reshape(RW_HEADS, RW_HD).astype(jnp.float32)
    kk = kk / jnp.maximum(jnp.sqrt(jnp.sum(kk * kk, axis=-1, keepdims=True)), 1e-12)
    k = k * (1.0 + (a - 1.0) * k_a.reshape(RW_HEADS, RW_HD).astype(jnp.float32))

    def step(S, inp):
        r_t, w_t, k_t, v_t, kk_t, b_t = inp
        sa = jnp.einsum('bhij,bhj->bhi', S, -kk_t)
        S = S * w_t[:, :, None, :] + sa[..., None] * b_t[:, :, None, :] + v_t[..., None] * k_t[:, :, None, :]
        return S, jnp.einsum('bhij,bhj->bhi', S, r_t)

    tm = lambda t: jnp.moveaxis(t, 1, 0)
    S, y = lax.scan(step, S0.astype(jnp.float32), (tm(r), tm(decay), tm(k), tm(v), tm(kk), tm(kk * a)))
    y = jnp.moveaxis(y, 0, 1)
    mean = jnp.mean(y, axis=-1, keepdims=True)
    var = jnp.mean(jnp.square(y - mean), axis=-1, keepdims=True)
    y = ((y - mean) * lax.rsqrt(var + RW_GN_EPS)).reshape(B, T, W)
    y = y * ln_w.astype(jnp.float32) + ln_b.astype(jnp.float32)
    bonus = jnp.sum(r * k * r_k.astype(jnp.float32), axis=-1, keepdims=True) * v
    y = (y + bonus.reshape(B, T, W)) * g.astype(jnp.float32)
    return y.astype(p.dtype), S, p[:, -1]


def peer_ffn(h, wq, sub_keys, u_tab, v_tab):
    B, T, D = h.shape
    n = B * T
    xt = h.reshape(n, D)
    n_pad = (-n) % PEER_CHUNK
    xc = jnp.pad(xt, ((0, n_pad), (0, 0))).reshape(-1, PEER_CHUNK, D)

    def chunk(xb):
        q = (xb @ wq).reshape(PEER_CHUNK, PEER_HEADS, 2, PEER_KD)
        s = jnp.einsum('thcd,hcnd->thcn', q.astype(jnp.float32), sub_keys.astype(jnp.float32))
        sv, si = lax.top_k(s, PEER_TOPK)
        cand = (sv[:, :, 0, :, None] + sv[:, :, 1, None, :]).reshape(PEER_CHUNK, PEER_HEADS, -1)
        cidx = (si[:, :, 0, :, None] * PEER_NKEYS + si[:, :, 1, None, :]).reshape(PEER_CHUNK, PEER_HEADS, -1)
        bv, bi = lax.top_k(cand, PEER_TOPK)
        eidx = jnp.take_along_axis(cidx, bi, axis=-1)
        gate = jax.nn.softmax(bv, axis=-1)
        u = u_tab[eidx]
        v = v_tab[eidx]
        act = jax.nn.gelu(jnp.einsum('thkd,td->thk', u, xb))
        return jnp.einsum('thk,thkd->td', (gate * act.astype(jnp.float32)).astype(xb.dtype), v)

    out = lax.map(chunk, xc).reshape(-1, D)[:n]
    return out.reshape(B, T, D)


def trunk_layer(x, pos, attend, wkv0, shift0, lam, lam_init, attn_norm, w_in, w_out, da_subln,
                rw_mu, rw_w0, rw_w_up, rw_a0, rw_a_up, rw_g_up, rw_k_k, rw_k_a, rw_r_k, rw_ln_w, rw_ln_b,
                ffn_norm, peer_wq, peer_sub_keys, peer_u, peer_v):
    B, T, _ = x.shape
    xn = rms_norm(x, attn_norm)
    proj = xn @ w_in
    q = rope(proj[..., :DA_QK].reshape(B, T, DA_HEADS, 2, DA_HD), pos)
    k = rope(proj[..., DA_QK:2 * DA_QK].reshape(B, T, DA_HEADS, 2, DA_HD), pos)
    v = proj[..., 2 * DA_QK:2 * DA_QK + DA_WIDTH].reshape(B, T, DA_HEADS, DA_VD)
    p_rw = proj[..., 2 * DA_QK + DA_WIDTH:]
    o_da = attend(q, k, v, lam)
    o_da = rms_norm(o_da, da_subln) * (1.0 - lam_init)
    o_rw, wkv, shift = rwkv_time_mix(p_rw, shift0, wkv0, rw_mu, rw_w0, rw_w_up, rw_a0, rw_a_up, rw_g_up,
                                     rw_k_k, rw_k_a, rw_r_k, rw_ln_w, rw_ln_b)
    mixed = jnp.concatenate([o_da.reshape(B, T, DA_WIDTH).astype(x.dtype), o_rw], axis=-1)
    h = x + mixed @ w_out
    h = h + peer_ffn(rms_norm(h, ffn_norm), peer_wq, peer_sub_keys, peer_u, peer_v)
    return h, k.reshape(B, T, DA_HEADS, 2 * DA_HD), v, wkv, shift


def setup_inputs(seed: int = 0) -> dict:
    key = jax.random.key(seed)
    ks = jax.random.split(key, 40)
    f32 = jnp.float32
    nrm = lambda kk, shape, s: jax.random.normal(kk, shape, f32) * s
    n_pages = PAST_LEN // PAGE_SIZE
    n_used = DEC_BATCH * n_pages
    n_pool = n_used + max(1, n_used // 4)
    page_table = jax.random.permutation(ks[0], n_pool)[:n_used].reshape(DEC_BATCH, n_pages).astype(jnp.int32)
    L = DEPTH
    return {
        'x_prompt': nrm(ks[1], (BATCH, SEQ, D_MODEL), 1.0),
        'x_sample': nrm(ks[2], (DEC_BATCH, DEC_SEQ, D_MODEL), 1.0),
        'cache_k': nrm(ks[3], (L, n_pool, PAGE_SIZE, DA_HEADS, 2 * DA_HD), 1.0),
        'cache_v': nrm(ks[4], (L, n_pool, PAGE_SIZE, DA_HEADS, DA_VD), 1.0),
        'state_wkv': nrm(ks[5], (L, DEC_BATCH, RW_HEADS, RW_HD, RW_HD), 0.5),
        'state_shift': nrm(ks[6], (L, DEC_BATCH, RW_PROJ), 1.0),
        'page_table': page_table,
        'attn_norm': 1.0 + nrm(ks[7], (L, D_MODEL), 0.01),
        'w_in': nrm(ks[8], (L, D_MODEL, IN_PROJ), D_MODEL ** -0.5),
        'w_out': nrm(ks[9], (L, MIX_WIDTH, D_MODEL), MIX_WIDTH ** -0.5),
        'da_lambda_q1': nrm(ks[10], (L, DA_HD), 0.1),
        'da_lambda_k1': nrm(ks[11], (L, DA_HD), 0.1),
        'da_lambda_q2': nrm(ks[12], (L, DA_HD), 0.1),
        'da_lambda_k2': nrm(ks[13], (L, DA_HD), 0.1),
        'da_subln': 1.0 + nrm(ks[14], (L, DA_VD), 0.01),
        'rw_mu': jax.random.uniform(ks[15], (L, RW_PROJ), f32, 0.0, 1.0),
        'rw_w0': jax.random.uniform(ks[16], (L, RW_WIDTH), f32, -6.0, 0.0),
        'rw_w_up': nrm(ks[17], (L, RW_W_RANK, RW_WIDTH), 0.5 * RW_W_RANK ** -0.5),
        'rw_a0': nrm(ks[18], (L, RW_WIDTH), 0.5),
        'rw_a_up': nrm(ks[19], (L, RW_A_RANK, RW_WIDTH), 0.5 * RW_A_RANK ** -0.5),
        'rw_g_up': nrm(ks[20], (L, RW_G_RANK, RW_WIDTH), RW_G_RANK ** -0.5),
        'rw_k_k': 0.85 + nrm(ks[21], (L, RW_WIDTH), 0.05),
        'rw_k_a': 1.0 + nrm(ks[22], (L, RW_WIDTH), 0.05),
        'rw_r_k': nrm(ks[23], (L, RW_HEADS, RW_HD), 0.1),
        'rw_ln_w': 1.0 + nrm(ks[24], (L, RW_WIDTH), 0.01),
        'rw_ln_b': nrm(ks[25], (L, RW_WIDTH), 0.01),
        'ffn_norm': 1.0 + nrm(ks[26], (L, D_MODEL), 0.01),
        'peer_wq': nrm(ks[27], (L, D_MODEL, PEER_HEADS * 2 * PEER_KD), D_MODEL ** -0.5),
        'peer_sub_keys': nrm(ks[28], (L, PEER_HEADS, 2, PEER_NKEYS, PEER_KD), PEER_KD ** -0.5),
        'peer_u': nrm(ks[29], (L, PEER_EXPERTS, D_MODEL), D_MODEL ** -0.5),
        'peer_v': nrm(ks[30], (L, PEER_EXPERTS, D_MODEL), PEER_HEADS ** -0.5),
        'final_norm': 1.0 + nrm(ks[31], (D_MODEL,), 0.01),
    }


def reference(x_prompt, x_sample, cache_k, cache_v, state_wkv, state_shift, page_table,
              attn_norm, w_in, w_out, da_lambda_q1, da_lambda_k1, da_lambda_q2, da_lambda_k2, da_subln,
              rw_mu, rw_w0, rw_w_up, rw_a0, rw_a_up, rw_g_up, rw_k_k, rw_k_a, rw_r_k, rw_ln_w, rw_ln_b,
              ffn_norm, peer_wq, peer_sub_keys, peer_u, peer_v, final_norm):
    B, T_p = x_prompt.shape[0], x_prompt.shape[1]
    T_s = x_sample.shape[1]
    past = page_table.shape[1] * PAGE_SIZE
    pos_p = jnp.arange(T_p, dtype=jnp.int32)
    pos_s = past + jnp.arange(T_s, dtype=jnp.int32)
    yp, ys = x_prompt, x_sample
    kp_l, vp_l, wp_l, sp_l, ks_l, vs_l, ws_l, ss_l = [], [], [], [], [], [], [], []
    for l in range(DEPTH):
        lam_init = 0.8 - 0.6 * math.exp(-0.3 * l)
        f = lambda t: t.astype(jnp.float32)
        lam = (jnp.exp(jnp.sum(f(da_lambda_q1[l]) * f(da_lambda_k1[l])))
               - jnp.exp(jnp.sum(f(da_lambda_q2[l]) * f(da_lambda_k2[l]))) + lam_init)
        shared = (attn_norm[l], w_in[l], w_out[l], da_subln[l], rw_mu[l], rw_w0[l], rw_w_up[l], rw_a0[l],
                  rw_a_up[l], rw_g_up[l], rw_k_k[l], rw_k_a[l], rw_r_k[l], rw_ln_w[l], rw_ln_b[l],
                  ffn_norm[l], peer_wq[l], peer_sub_keys[l], peer_u[l], peer_v[l])
        att_p = functools.partial(attend_prompt, pos=pos_p)
        att_s = functools.partial(attend_sample, pos=pos_s, k_cache=cache_k[l], v_cache=cache_v[l],
                                  page_table=page_table)
        wkv0_p = jnp.zeros((B, RW_HEADS, RW_HD, RW_HD), jnp.float32)
        shift0_p = jnp.zeros((B, RW_PROJ), x_prompt.dtype)
        yp, kp, vp, wp, sp = trunk_layer(yp, pos_p, att_p, wkv0_p, shift0_p, lam, lam_init, *shared)
        ys, ks_, vs, ws, ss = trunk_layer(ys, pos_s, att_s, state_wkv[l], state_shift[l], lam, lam_init, *shared)
        kp_l.append(kp)
        vp_l.append(vp)
        wp_l.append(wp.astype(x_prompt.dtype))
        sp_l.append(sp)
        ks_l.append(ks_.astype(cache_k.dtype))
        vs_l.append(vs.astype(cache_v.dtype))
        ws_l.append(ws.astype(state_wkv.dtype))
        ss_l.append(ss.astype(state_shift.dtype))
    y_prompt = rms_norm(yp, final_norm)
    y_sample = rms_norm(ys, final_norm)
    k_prompt = jnp.stack(kp_l)
    v_prompt = jnp.stack(vp_l)
    wkv_prompt = jnp.stack(wp_l)
    shift_prompt = jnp.stack(sp_l)
    k_sample = jnp.stack(ks_l)
    v_sample = jnp.stack(vs_l)
    wkv_sample = jnp.stack(ws_l)
    shift_sample = jnp.stack(ss_l)
    return (y_prompt, y_sample, k_prompt, v_prompt, wkv_prompt, shift_prompt, k_sample, v_sample, wkv_sample, shift_sample)
```

```python
import functools
import math

import jax
import jax.numpy as jnp
from jax import lax
from jax.experimental import pallas as pl
from jax.experimental.pallas import tpu as pltpu

F32 = jnp.float32
BF16 = jnp.bfloat16

D_MODEL = 1024
PAGE_SIZE = 128
DA_HEADS = 4
DA_HD = 64
DA_VD = 2 * DA_HD
DA_WIDTH = DA_HEADS * DA_VD
DA_QK = DA_HEADS * 2 * DA_HD
RW_HEADS = 8
RW_HD = 64
RW_WIDTH = RW_HEADS * RW_HD
RW_W_RANK = 64
RW_A_RANK = 64
RW_G_RANK = 128
RW_LORA = RW_W_RANK + RW_A_RANK + RW_G_RANK
RW_PROJ = 3 * RW_WIDTH + RW_LORA
RW_GN_EPS = 64e-5
IN_PROJ = 2 * DA_QK + DA_WIDTH + RW_PROJ
PEER_HEADS = 8
PEER_NKEYS = 128
PEER_KD = 128
PEER_TOPK = 16
PEER_PICKS = PEER_HEADS * PEER_TOPK
PEER_QW = PEER_HEADS * 2 * PEER_KD
ROPE_THETA = 10000.0
NORM_EPS = 1e-6
NEG_INF = -1e30
LAM_INIT = 0.8 - 0.6 * math.exp(-0.3 * 0)

LANES = 128
SUBLANES = 8
VMEM_LIMIT_BYTES = 48 * 1024 * 1024

HIGHEST = lax.Precision.HIGHEST


def _cparams(sem):
    return pltpu.CompilerParams(dimension_semantics=sem, vmem_limit_bytes=VMEM_LIMIT_BYTES)


def _full(shape):
    return pl.BlockSpec(shape, lambda *_: (0,) * len(shape))


def _rms(x, g):
    return x * lax.rsqrt(jnp.mean(x * x, axis=-1, keepdims=True) + NORM_EPS) * g


def _sigmoid(x):
    return 1.0 / (1.0 + jnp.exp(-x))


def _rope_slab(t, cos, sin_signed):
    lane = lax.broadcasted_iota(jnp.int32, t.shape, 1)
    swapped = jnp.where(lane % DA_HD < DA_HD // 2,
                        pltpu.roll(t, LANES - DA_HD // 2, 1),
                        pltpu.roll(t, DA_HD // 2, 1))
    return t * cos + swapped * sin_signed


def _in_proj_kernel(x_ref, g_ref, w_ref, cos_ref, sin_ref, q_ref, k_ref, v_ref, p_ref):
    xn = _rms(x_ref[...], g_ref[...])
    proj = jnp.dot(xn.astype(BF16), w_ref[...], preferred_element_type=F32)
    cos = cos_ref[...]
    sin = sin_ref[...]
    for s in range(DA_QK // LANES):
        lo = s * LANES
        q_ref[:, lo:lo + LANES] = _rope_slab(proj[:, lo:lo + LANES], cos, sin)
        k_ref[:, lo:lo + LANES] = _rope_slab(proj[:, DA_QK + lo:DA_QK + lo + LANES], cos, sin)
    v_ref[...] = proj[:, 2 * DA_QK:2 * DA_QK + DA_WIDTH]
    p_ref[...] = proj[:, 2 * DA_QK + DA_WIDTH:]


def _in_proj(x, g, w_bf, cos, sin, tm):
    n = x.shape[0]
    row = lambda w: pl.BlockSpec((tm, w), lambda i: (i, 0))
    return pl.pallas_call(
        _in_proj_kernel,
        grid=(n // tm,),
        in_specs=[row(D_MODEL), _full((1, D_MODEL)), _full((D_MODEL, IN_PROJ)), row(LANES), row(LANES)],
        out_specs=[row(DA_QK), row(DA_QK), row(DA_WIDTH), row(RW_PROJ)],
        out_shape=[jax.ShapeDtypeStruct((n, DA_QK), F32), jax.ShapeDtypeStruct((n, DA_QK), F32),
                   jax.ShapeDtypeStruct((n, DA_WIDTH), F32), jax.ShapeDtypeStruct((n, RW_PROJ), F32)],
        compiler_params=_cparams(("parallel",)),
        name="in_proj",
    )(x, g, w_bf, cos, sin)


def _rope_tables(pos):
    half = DA_HD // 2
    inv = ROPE_THETA ** (-jnp.arange(half, dtype=F32) / half)
    ang = pos.astype(F32)[:, None] * inv[None, :]
    cos, sin = jnp.cos(ang), jnp.sin(ang)
    cos = jnp.tile(cos, (1, LANES // half))
    sin = jnp.tile(jnp.concatenate([-sin, sin], axis=1), (1, LANES // DA_HD))
    return cos, sin


def _lambda(lq1, lk1, lq2, lk2):
    return (jnp.exp(jnp.sum(lq1 * lk1, axis=-1, keepdims=True))
            - jnp.exp(jnp.sum(lq2 * lk2, axis=-1, keepdims=True)) + LAM_INIT)


def _online(s, vb, m_ref, l_ref, acc_ref):
    m_old = m_ref[...]
    m_new = jnp.maximum(m_old, jnp.max(s, axis=-1, keepdims=True))
    alpha = jnp.exp(m_old - m_new)
    p = jnp.exp(s - m_new)
    l_ref[...] = alpha * l_ref[...] + jnp.sum(p, axis=-1, keepdims=True)
    acc_ref[...] = alpha * acc_ref[...] + jnp.dot(p.astype(BF16), vb, preferred_element_type=F32)
    m_ref[...] = m_new


def _attn_prompt_kernel(q_ref, k_ref, v_ref, lq1, lk1, lq2, lk2, g_ref, o_ref,
                        m1, l1, a1, m2, l2, a2, *, tq):
    qi = pl.program_id(2)
    ki = pl.program_id(3)

    @pl.when(ki == 0)
    def _():
        for m, l, a in ((m1, l1, a1), (m2, l2, a2)):
            m[...] = jnp.full(m.shape, NEG_INF, F32)
            l[...] = jnp.zeros(l.shape, F32)
            a[...] = jnp.zeros(a.shape, F32)

    @pl.when(ki <= qi)
    def _():
        q = q_ref[0] * (DA_HD ** -0.5)
        lane = lax.broadcasted_iota(jnp.int32, q.shape, 1)
        q1 = jnp.where(lane < DA_HD, q, 0.0).astype(BF16)
        q2 = jnp.where(lane < DA_HD, 0.0, q).astype(BF16)
        kb = k_ref[0].astype(BF16)
        vb = v_ref[0].astype(BF16)
        row = lax.broadcasted_iota(jnp.int32, (tq, tq), 0) + qi * tq
        col = lax.broadcasted_iota(jnp.int32, (tq, tq), 1) + ki * tq
        mask = col <= row
        dn = (((1,), (1,)), ((), ()))
        s1 = jnp.where(mask, lax.dot_general(q1, kb, dn, preferred_element_type=F32), NEG_INF)
        _online(s1, vb, m1, l1, a1)
        s2 = jnp.where(mask, lax.dot_general(q2, kb, dn, preferred_element_type=F32), NEG_INF)
        _online(s2, vb, m2, l2, a2)

    @pl.when(ki == qi)
    def _():
        lam = _lambda(lq1[...], lk1[...], lq2[...], lk2[...])
        o = a1[...] / l1[...] - lam * (a2[...] / l2[...])
        o_ref[0] = _rms(o, g_ref[...]) * (1.0 - LAM_INIT)


def _attn_prompt(q, k, v, lams, subln, tq):
    b, t, _ = q.shape
    nq = t // tq
    qspec = pl.BlockSpec((1, tq, DA_VD), lambda bi, h, qi, ki: (bi, qi, h))
    kspec = pl.BlockSpec((1, tq, DA_VD), lambda bi, h, qi, ki: (bi, jnp.minimum(ki, qi), h))
    vec = lambda w: _full((1, w))
    return pl.pallas_call(
        functools.partial(_attn_prompt_kernel, tq=tq),
        grid=(b, DA_HEADS, nq, nq),
        in_specs=[qspec, kspec, kspec, vec(DA_HD), vec(DA_HD), vec(DA_HD), vec(DA_HD), vec(DA_VD)],
        out_specs=qspec,
        out_shape=jax.ShapeDtypeStruct((b, t, DA_WIDTH), F32),
        scratch_shapes=[pltpu.VMEM((tq, 1), F32), pltpu.VMEM((tq, 1), F32), pltpu.VMEM((tq, DA_VD), F32),
                        pltpu.VMEM((tq, 1), F32), pltpu.VMEM((tq, 1), F32), pltpu.VMEM((tq, DA_VD), F32)],
        compiler_params=_cparams(("parallel", "parallel", "parallel", "arbitrary")),
        name="attn_prompt",
    )(q, k, v, *lams, subln)


def _attn_sample_kernel(pt_ref, q_ref, ck_ref, cv_ref, kn_ref, vn_ref, lq1, lk1, lq2, lk2, g_ref, o_ref,
                        qb_ref, m_ref, l_ref, acc_ref, *, n_pages, t_new):
    p = pl.program_id(1)
    n_rows = DA_HEADS * 2 * t_new

    @pl.when(p == 0)
    def _():
        q = q_ref[0] * (DA_HD ** -0.5)
        lane = lax.broadcasted_iota(jnp.int32, q.shape, 1)
        for hc in range(DA_HEADS * 2):
            sel = (lane >= hc * DA_HD) & (lane < (hc + 1) * DA_HD)
            qb_ref[hc * t_new:(hc + 1) * t_new, :] = jnp.where(sel, q, 0.0)
        m_ref[...] = jnp.full(m_ref.shape, NEG_INF, F32)
        l_ref[...] = jnp.zeros(l_ref.shape, F32)
        acc_ref[...] = jnp.zeros(acc_ref.shape, F32)

    @pl.when(p < n_pages)
    def _():
        dn = (((1,), (1,)), ((), ()))
        s = lax.dot_general(qb_ref[...].astype(BF16), ck_ref[0].astype(BF16), dn, preferred_element_type=F32)
        _online(s, cv_ref[0].astype(BF16), m_ref, l_ref, acc_ref)

    @pl.when(p == n_pages)
    def _():
        qb = qb_ref[...]
        kn = kn_ref[0]
        vn = vn_ref[0]
        t_of_row = lax.broadcasted_iota(jnp.int32, (n_rows, 1), 0) % t_new
        s_new = []
        for j in range(t_new):
            s_j = jnp.sum(qb * kn[j:j + 1, :], axis=-1, keepdims=True)
            s_new.append(jnp.where(t_of_row >= j, s_j, NEG_INF))
        m_old = m_ref[...]
        m_new = m_old
        for s_j in s_new:
            m_new = jnp.maximum(m_new, s_j)
        alpha = jnp.exp(m_old - m_new)
        l = alpha * l_ref[...]
        acc = alpha * acc_ref[...]
        for j, s_j in enumerate(s_new):
            p_j = jnp.exp(s_j - m_new)
            l = l + p_j
            acc = acc + p_j * vn[j:j + 1, :]
        o_all = acc / l
        lam = _lambda(lq1[...], lk1[...], lq2[...], lk2[...])
        for h in range(DA_HEADS):
            r1 = (2 * h) * t_new
            r2 = (2 * h + 1) * t_new
            lo = h * DA_VD
            o = o_all[r1:r1 + t_new, lo:lo + DA_VD] - lam * o_all[r2:r2 + t_new, lo:lo + DA_VD]
            o_ref[0, :, lo:lo + DA_VD] = _rms(o, g_ref[...]) * (1.0 - LAM_INIT)


def _attn_sample(q, k_new, v_new, cache_k, cache_v, page_table, lams, subln):
    bd, t_new, _ = q.shape
    n_pages = page_table.shape[1]
    n_rows = DA_HEADS * 2 * t_new
    tok = pl.BlockSpec((1, t_new, DA_WIDTH), lambda b, p, pt: (b, 0, 0))
    page = pl.BlockSpec((1, PAGE_SIZE, DA_WIDTH), lambda b, p, pt: (pt[b, jnp.minimum(p, n_pages - 1)], 0, 0))
    vec = lambda w: pl.BlockSpec((1, w), lambda b, p, pt: (0, 0))
    grid_spec = pltpu.PrefetchScalarGridSpec(
        num_scalar_prefetch=1,
        grid=(bd, n_pages + 1),
        in_specs=[tok, page, page, tok, tok, vec(DA_HD), vec(DA_HD), vec(DA_HD), vec(DA_HD), vec(DA_VD)],
        out_specs=tok,
        scratch_shapes=[pltpu.VMEM((n_rows, DA_WIDTH), F32), pltpu.VMEM((n_rows, 1), F32),
                        pltpu.VMEM((n_rows, 1), F32), pltpu.VMEM((n_rows, DA_WIDTH), F32)],
    )
    return pl.pallas_call(
        functools.partial(_attn_sample_kernel, n_pages=n_pages, t_new=t_new),
        grid_spec=grid_spec,
        out_shape=jax.ShapeDtypeStruct((bd, t_new, DA_WIDTH), F32),
        compiler_params=_cparams(("parallel", "arbitrary")),
        name="attn_sample",
    )(page_table, q, cache_k, cache_v, k_new, v_new, *lams, subln)


def _head_sum(x, seg):
    return jnp.dot(x, seg, preferred_element_type=F32, precision=HIGHEST)


def _rw_pre_kernel(p_ref, first_ref, mu_ref, wcat_ref, w0_ref, a0_ref, kk_ref, ka_ref, rk_ref, seg_ref,
                   r_out, w_out, k_out, v_out, kk_out, b_out, g_out, bonus_out):
    p = p_ref[...]
    gb, rows, _ = p.shape
    row = lax.broadcasted_iota(jnp.int32, p.shape, 1)
    p_prev = jnp.where(row == 0, first_ref[...], pltpu.roll(p, 1, 1))
    ps = (p + (p_prev - p) * mu_ref[...]).reshape(gb * rows, RW_PROJ)
    w_ = RW_WIDTH
    r, k, v = ps[:, :w_], ps[:, w_:2 * w_], ps[:, 2 * w_:3 * w_]
    x = ps[:, 3 * w_:]
    lane = lax.broadcasted_iota(jnp.int32, x.shape, 1)
    act = jnp.where(lane < RW_W_RANK, jnp.tanh(x), jnp.where(lane < RW_W_RANK + RW_A_RANK, x, _sigmoid(x)))
    lora = jnp.dot(act.astype(BF16), wcat_ref[...], preferred_element_type=F32)
    decay = jnp.exp(-math.exp(-0.5) * _sigmoid(w0_ref[...] + lora[:, :w_]))
    a = _sigmoid(a0_ref[...] + lora[:, w_:2 * w_])
    seg = seg_ref[...]
    kk = k * kk_ref[...]
    kk = kk / jnp.maximum(jnp.sqrt(_head_sum(kk * kk, seg)), 1e-12)
    k_mod = k * (1.0 + (a - 1.0) * ka_ref[...])
    r_out[...] = r
    w_out[...] = decay
    k_out[...] = k_mod
    v_out[...] = v
    kk_out[...] = kk
    b_out[...] = kk * a
    g_out[...] = lora[:, 2 * w_:]
    bonus_out[...] = _head_sum(r * k_mod * rk_ref[...], seg) * v


def _rw_pre(p3, first, mu, wcat, w0, a0, k_k, k_a, r_k, seg, gb):
    g, rows, _ = p3.shape
    n = g * rows
    vec = lambda w: _full((1, w))
    out = pl.BlockSpec((gb * rows, RW_WIDTH), lambda i: (i, 0))
    return pl.pallas_call(
        _rw_pre_kernel,
        grid=(g // gb,),
        in_specs=[pl.BlockSpec((gb, rows, RW_PROJ), lambda i: (i, 0, 0)),
                  pl.BlockSpec((gb, 1, RW_PROJ), lambda i: (i, 0, 0)),
                  vec(RW_PROJ), _full((RW_LORA, 3 * RW_WIDTH)), vec(RW_WIDTH), vec(RW_WIDTH),
                  vec(RW_WIDTH), vec(RW_WIDTH), vec(RW_WIDTH), _full((RW_WIDTH, RW_WIDTH))],
        out_specs=[out] * 8,
        out_shape=[jax.ShapeDtypeStruct((n, RW_WIDTH), F32)] * 8,
        compiler_params=_cparams(("parallel",)),
        name="rw_pre",
    )(p3, first, mu, wcat, w0, a0, k_k, k_a, r_k, seg)


def _rw_scan_kernel(r_ref, w_ref, k_ref, v_ref, kk_ref, b_ref, s0_ref, y_ref, s_ref, *, bb, tc):
    @pl.when(pl.program_id(1) == 0)
    def _():
        s_ref[...] = s0_ref[...]

    ii = lax.broadcasted_iota(jnp.int32, (RW_HD, RW_HD), 0)
    jj = lax.broadcasted_iota(jnp.int32, (RW_HD, RW_HD), 1)
    eye = (ii == jj).astype(F32)

    def step(t, carry):
        for b in range(bb):
            for h in range(RW_HEADS):
                row = lambda ref: ref[b, t, pl.ds(h, 1), :]
                s = s_ref[b, h]
                sa = -jnp.sum(s * row(kk_ref), axis=-1, keepdims=True)
                v_col = jnp.sum(eye * row(v_ref), axis=-1, keepdims=True)
                s = s * row(w_ref) + sa * row(b_ref) + v_col * row(k_ref)
                s_ref[b, h] = s
                y_col = jnp.sum(s * row(r_ref), axis=-1, keepdims=True)
                y_ref[b, t, pl.ds(h, 1), :] = jnp.sum(eye * y_col, axis=0, keepdims=True)
        return carry

    lax.fori_loop(0, tc, step, 0)


def _rw_scan(r, w, k, v, kk, b, s0, bb, tc):
    bsz, t = r.shape[0], r.shape[1]
    tok = pl.BlockSpec((bb, tc, RW_HEADS, RW_HD), lambda i, c: (i, c, 0, 0))
    st = pl.BlockSpec((bb, RW_HEADS, RW_HD, RW_HD), lambda i, c: (i, 0, 0, 0))
    return pl.pallas_call(
        functools.partial(_rw_scan_kernel, bb=bb, tc=tc),
        grid=(bsz // bb, t // tc),
        in_specs=[tok] * 6 + [st],
        out_specs=[tok, st],
        out_shape=[jax.ShapeDtypeStruct((bsz, t, RW_HEADS, RW_HD), F32),
                   jax.ShapeDtypeStruct((bsz, RW_HEADS, RW_HD, RW_HD), F32)],
        compiler_params=_cparams(("parallel", "arbitrary")),
        name="rw_scan",
    )(r, w, k, v, kk, b, s0)


def _out_proj_kernel(x_ref, oda_ref, y_ref, bonus_ref, g_ref, lnw_ref, lnb_ref, seg_ref, wo_ref, fn_ref, wq_ref,
                     h_ref, hn_ref, qp_ref):
    seg = seg_ref[...]
    y = y_ref[...]
    d = y - _head_sum(y, seg) * (1.0 / RW_HD)
    var = _head_sum(d * d, seg) * (1.0 / RW_HD)
    yn = d * lax.rsqrt(var + RW_GN_EPS) * lnw_ref[...] + lnb_ref[...]
    o_rw = (yn + bonus_ref[...]) * g_ref[...]
    h = (x_ref[...]
         + jnp.dot(oda_ref[...].astype(BF16), wo_ref[:DA_WIDTH, :], preferred_element_type=F32)
         + jnp.dot(o_rw.astype(BF16), wo_ref[DA_WIDTH:, :], preferred_element_type=F32))
    hn = _rms(h, fn_ref[...])
    h_ref[...] = h
    hn_ref[...] = hn
    qp_ref[...] = jnp.dot(hn.astype(BF16), wq_ref[...], preferred_element_type=F32)


def _out_proj(x, o_da, y, bonus, g, ln_w, ln_b, seg, w_out_bf, ffn_norm, wq_bf, tm):
    n = x.shape[0]
    row = lambda w: pl.BlockSpec((tm, w), lambda i: (i, 0))
    vec = lambda w: _full((1, w))
    return pl.pallas_call(
        _out_proj_kernel,
        grid=(n // tm,),
        in_specs=[row(D_MODEL), row(DA_WIDTH), row(RW_WIDTH), row(RW_WIDTH), row(RW_WIDTH),
                  vec(RW_WIDTH), vec(RW_WIDTH), _full((RW_WIDTH, RW_WIDTH)),
                  _full((DA_WIDTH + RW_WIDTH, D_MODEL)), vec(D_MODEL), _full((D_MODEL, PEER_QW))],
        out_specs=[row(D_MODEL), row(D_MODEL), row(PEER_QW)],
        out_shape=[jax.ShapeDtypeStruct((n, D_MODEL), F32), jax.ShapeDtypeStruct((n, D_MODEL), F32),
                   jax.ShapeDtypeStruct((n, PEER_QW), F32)],
        compiler_params=_cparams(("parallel",)),
        name="out_proj",
    )(x, o_da, y, bonus, g, ln_w, ln_b, seg, w_out_bf, ffn_norm, wq_bf)


def _top_k_rows(s, payload, k):
    n = s.shape[0]
    row = lax.broadcasted_iota(jnp.int32, s.shape, 0).astype(F32)
    vals, pays = [], []
    for _ in range(k):
        m = jnp.max(s, axis=0, keepdims=True)
        first = jnp.min(jnp.where(s == m, row, float(n)), axis=0, keepdims=True)
        hit = row == first
        vals.append(m)
        pays.append(jnp.max(jnp.where(hit, payload, -1.0), axis=0, keepdims=True))
        s = jnp.where(hit, -jnp.inf, s)
    return jnp.concatenate(vals, axis=0), jnp.concatenate(pays, axis=0)


def _peer_topk_kernel(qp_ref, keys_ref, idx_ref, gate_ref, *, tm):
    key_row = lax.broadcasted_iota(jnp.int32, (PEER_NKEYS, tm), 0).astype(F32)
    dn = (((1,), (1,)), ((), ()))
    idx_rows, gate_rows = [], []
    for h in range(PEER_HEADS):
        sv, si = [], []
        for c in range(2):
            hc = 2 * h + c
            q = qp_ref[:, hc * PEER_KD:(hc + 1) * PEER_KD].astype(BF16)
            s = lax.dot_general(keys_ref[hc], q, dn, preferred_element_type=F32)
            v_, i_ = _top_k_rows(s, key_row, PEER_TOPK)
            sv.append(v_)
            si.append(i_)
        cand = jnp.concatenate([sv[0][a:a + 1, :] + sv[1] for a in range(PEER_TOPK)], axis=0)
        cidx = jnp.concatenate([si[0][a:a + 1, :] * PEER_NKEYS + si[1] for a in range(PEER_TOPK)], axis=0)
        bv, eidx = _top_k_rows(cand, cidx, PEER_TOPK)
        e = jnp.exp(bv - bv[0:1, :])
        gate_rows.append(e / jnp.sum(e, axis=0, keepdims=True))
        idx_rows.append(eidx)
    idx_ref[...] = jnp.concatenate(idx_rows, axis=0).T.astype(jnp.int32)
    gate_ref[...] = jnp.concatenate(gate_rows, axis=0).T


def _peer_topk(qp, keys_bf, tm):
    n = qp.shape[0]
    return pl.pallas_call(
        functools.partial(_peer_topk_kernel, tm=tm),
        grid=(n // tm,),
        in_specs=[pl.BlockSpec((tm, PEER_QW), lambda i: (i, 0)),
                  _full((PEER_HEADS * 2, PEER_NKEYS, PEER_KD))],
        out_specs=[pl.BlockSpec((tm, PEER_PICKS), lambda i: (i, 0))] * 2,
        out_shape=[jax.ShapeDtypeStruct((n, PEER_PICKS), jnp.int32),
                   jax.ShapeDtypeStruct((n, PEER_PICKS), F32)],
        compiler_params=_cparams(("parallel",)),
        name="peer_topk",
    )(qp, keys_bf)


def _peer_expert_kernel(idx_ref, hn_ref, h_ref, gate_ref, fn_ref, u_hbm, v_hbm, out_ref,
                        ubuf, vbuf, sems, ffn_ref, *, tt):
    ii = lax.broadcasted_iota(jnp.int32, (PEER_PICKS, PEER_PICKS), 0)
    jj = lax.broadcasted_iota(jnp.int32, (PEER_PICKS, PEER_PICKS), 1)
    eye = (ii == jj).astype(F32)

    def row_copies(t, slot, j):
        e = idx_ref[t, j]
        return (pltpu.make_async_copy(u_hbm.at[pl.ds(e, 1)], ubuf.at[slot, pl.ds(j, 1)], sems.at[0, slot]),
                pltpu.make_async_copy(v_hbm.at[pl.ds(e, 1)], vbuf.at[slot, pl.ds(j, 1)], sems.at[1, slot]))

    def start_token(t, slot):
        def body(j, c):
            cu, cv = row_copies(t, slot, j)
            cu.start()
            cv.start()
            return c
        lax.fori_loop(0, PEER_PICKS, body, 0)

    def wait_token(slot):
        pltpu.make_async_copy(u_hbm.at[pl.ds(0, PEER_PICKS)], ubuf.at[slot], sems.at[0, slot]).wait()
        pltpu.make_async_copy(v_hbm.at[pl.ds(0, PEER_PICKS)], vbuf.at[slot], sems.at[1, slot]).wait()

    start_token(0, 0)

    def token(t, carry):
        slot = t % 2

        @pl.when(t + 1 < tt)
        def _():
            start_token(t + 1, 1 - slot)

        wait_token(slot)
        x = hn_ref[pl.ds(t, 1), :]
        lin = jnp.sum(ubuf[slot] * x, axis=-1, keepdims=True)
        act = jax.nn.gelu(lin)
        gate_col = jnp.sum(eye * gate_ref[pl.ds(t, 1), :], axis=-1, keepdims=True)
        ffn_ref[pl.ds(t, 1), :] = jnp.sum(vbuf[slot] * (gate_col * act), axis=0, keepdims=True)
        return carry

    lax.fori_loop(0, tt, token, 0)
    out_ref[...] = _rms(h_ref[...] + ffn_ref[...], fn_ref[...])


def _peer_expert(idx, hn, h, gate, final_norm, u_tab, v_tab, tt):
    n = hn.shape[0]
    row = lambda w: pl.BlockSpec((tt, w), lambda i: (i, 0))
    return pl.pallas_call(
        functools.partial(_peer_expert_kernel, tt=tt),
        grid=(n // tt,),
        in_specs=[pl.BlockSpec((tt, PEER_PICKS), lambda i: (i, 0), memory_space=pltpu.SMEM),
                  row(D_MODEL), row(D_MODEL), row(PEER_PICKS), _full((1, D_MODEL)),
                  pl.BlockSpec(memory_space=pl.ANY), pl.BlockSpec(memory_space=pl.ANY)],
        out_specs=row(D_MODEL),
        out_shape=jax.ShapeDtypeStruct((n, D_MODEL), F32),
        scratch_shapes=[pltpu.VMEM((2, PEER_PICKS, D_MODEL), F32), pltpu.VMEM((2, PEER_PICKS, D_MODEL), F32),
                        pltpu.SemaphoreType.DMA((2, 2)), pltpu.VMEM((tt, D_MODEL), F32)],
        compiler_params=_cparams(("arbitrary",)),
        name="peer_expert",
    )(idx, hn, h, gate, final_norm, u_tab, v_tab)


def _row_tile(n, target):
    tm = min(n, target)
    while n % tm:
        tm //= 2
    return tm


def _group(x, pos, attend, wkv0, shift0, wts):
    b, t, _ = x.shape
    n = b * t
    x2 = x.reshape(n, D_MODEL)
    tm = _row_tile(n, 256)
    cos, sin = _rope_tables(jnp.tile(pos, b))
    q, k, v, p_rw = _in_proj(x2, wts["attn_norm"], wts["w_in"], cos, sin, tm)

    o_da = attend(q.reshape(b, t, DA_QK), k.reshape(b, t, DA_QK), v.reshape(b, t, DA_WIDTH))

    rows = _row_tile(t, 256)
    p3 = p_rw.reshape(n // rows, rows, RW_PROJ)
    p_bt = p_rw.reshape(b, t, RW_PROJ)
    prev_rows = p_bt[:, rows - 1::rows][:, :t // rows - 1]
    first = jnp.concatenate([shift0[:, None], prev_rows], axis=1).reshape(n // rows, 1, RW_PROJ)
    gb = _row_tile(n // rows, max(1, 256 // rows))
    r, w, k_mod, v_rw, kk, bb_, g, bonus = _rw_pre(
        p3, first, wts["rw_mu"], wts["rw_wcat"], wts["rw_w0"], wts["rw_a0"],
        wts["rw_k_k"], wts["rw_k_a"], wts["rw_r_k"], wts["seg"], gb)
    heads = lambda a: a.reshape(b, t, RW_HEADS, RW_HD)
    y, wkv = _rw_scan(heads(r), heads(w), heads(k_mod), heads(v_rw), heads(kk), heads(bb_), wkv0,
                      bb=_row_tile(b, 4), tc=_row_tile(t, 64))

    h, hn, qp = _out_proj(x2, o_da.reshape(n, DA_WIDTH), y.reshape(n, RW_WIDTH), bonus, g,
                          wts["rw_ln_w"], wts["rw_ln_b"], wts["seg"], wts["w_out"], wts["ffn_norm"],
                          wts["peer_wq"], tm)
    idx, gate = _peer_topk(qp, wts["peer_keys"], tm)
    out = _peer_expert(idx, hn, h, gate, wts["final_norm"], wts["peer_u"], wts["peer_v"], _row_tile(n, 64))
    return (out.reshape(b, t, D_MODEL), k.reshape(b, t, DA_HEADS, 2 * DA_HD), v.reshape(b, t, DA_HEADS, DA_VD),
            wkv, p_bt[:, -1])


def kernel(x_prompt, x_sample, cache_k, cache_v, state_wkv, state_shift, page_table, attn_norm, w_in, w_out,
           da_lambda_q1, da_lambda_k1, da_lambda_q2, da_lambda_k2, da_subln, rw_mu, rw_w0, rw_w_up, rw_a0,
           rw_a_up, rw_g_up, rw_k_k, rw_k_a, rw_r_k, rw_ln_w, rw_ln_b, ffn_norm, peer_wq, peer_sub_keys,
           peer_u, peer_v, final_norm):
    assert w_in.shape[0] == 1, "single-layer trunk"
    b, t_p, _ = x_prompt.shape
    bd, t_s, _ = x_sample.shape
    past = page_table.shape[1] * PAGE_SIZE
    vec = lambda a: a.reshape(1, -1).astype(F32)

    wcat = jnp.zeros((RW_LORA, 3 * RW_WIDTH), F32)
    wcat = wcat.at[:RW_W_RANK, :RW_WIDTH].set(rw_w_up[0])
    wcat = wcat.at[RW_W_RANK:RW_W_RANK + RW_A_RANK, RW_WIDTH:2 * RW_WIDTH].set(rw_a_up[0])
    wcat = wcat.at[RW_W_RANK + RW_A_RANK:, 2 * RW_WIDTH:].set(rw_g_up[0])
    lane_head = jnp.arange(RW_WIDTH) // RW_HD
    wts = dict(
        attn_norm=vec(attn_norm[0]), w_in=w_in[0].astype(BF16), w_out=w_out[0].astype(BF16),
        rw_mu=vec(rw_mu[0]), rw_wcat=wcat.astype(BF16), rw_w0=vec(rw_w0[0]), rw_a0=vec(rw_a0[0]),
        rw_k_k=vec(rw_k_k[0]), rw_k_a=vec(rw_k_a[0]), rw_r_k=vec(rw_r_k[0]),
        rw_ln_w=vec(rw_ln_w[0]), rw_ln_b=vec(rw_ln_b[0]),
        seg=(lane_head[:, None] == lane_head[None, :]).astype(F32),
        ffn_norm=vec(ffn_norm[0]), peer_wq=peer_wq[0].astype(BF16),
        peer_keys=peer_sub_keys[0].reshape(PEER_HEADS * 2, PEER_NKEYS, PEER_KD).astype(BF16),
        peer_u=peer_u[0], peer_v=peer_v[0], final_norm=vec(final_norm),
    )
    lams = (vec(da_lambda_q1[0]), vec(da_lambda_k1[0]), vec(da_lambda_q2[0]), vec(da_lambda_k2[0]))
    subln = vec(da_subln[0])

    att_p = lambda q, k, v: _attn_prompt(q, k, v, lams, subln, _row_tile(t_p, 512))
    ck = cache_k[0].reshape(-1, PAGE_SIZE, DA_QK)
    cv = cache_v[0].reshape(-1, PAGE_SIZE, DA_WIDTH)
    att_s = lambda q, k, v: _attn_sample(q, k, v, ck, cv, page_table, lams, subln)

    pos_p = jnp.arange(t_p, dtype=jnp.int32)
    pos_s = past + jnp.arange(t_s, dtype=jnp.int32)
    yp, kp, vp, wp, sp = _group(x_prompt, pos_p, att_p, jnp.zeros((b, RW_HEADS, RW_HD, RW_HD), F32),
                                jnp.zeros((b, RW_PROJ), F32), wts)
    ys, ks, vs, ws, ss = _group(x_sample, pos_s, att_s, state_wkv[0], state_shift[0], wts)
    return (yp, ys, kp[None], vp[None], wp[None], sp[None], ks[None], vs[None], ws[None], ss[None])
```

```python
import functools
import math

import jax
import jax.numpy as jnp
from jax import lax
from jax.experimental import pallas as pl
from jax.experimental.pallas import tpu as pltpu

F32 = jnp.float32
BF16 = jnp.bfloat16

D_MODEL = 1024
PAGE_SIZE = 128
DA_HEADS = 4
DA_HD = 64
DA_VD = 2 * DA_HD
DA_WIDTH = DA_HEADS * DA_VD
DA_QK = DA_HEADS * 2 * DA_HD
RW_HEADS = 8
RW_HD = 64
RW_WIDTH = RW_HEADS * RW_HD
RW_W_RANK = 64
RW_A_RANK = 64
RW_G_RANK = 128
RW_LORA = RW_W_RANK + RW_A_RANK + RW_G_RANK
RW_PROJ = 3 * RW_WIDTH + RW_LORA
RW_GN_EPS = 64e-5
IN_PROJ = 2 * DA_QK + DA_WIDTH + RW_PROJ
PEER_HEADS = 8
PEER_NKEYS = 128
PEER_KD = 128
PEER_TOPK = 16
PEER_PICKS = PEER_HEADS * PEER_TOPK
PEER_QW = PEER_HEADS * 2 * PEER_KD
ROPE_THETA = 10000.0
NORM_EPS = 1e-6
NEG_INF = -1e30
LAM_INIT = 0.8 - 0.6 * math.exp(-0.3 * 0)

LANES = 128
SUBLANES = 8
VMEM_LIMIT_BYTES = 48 * 1024 * 1024

HIGHEST = lax.Precision.HIGHEST


def _cparams(sem):
    return pltpu.CompilerParams(dimension_semantics=sem, vmem_limit_bytes=VMEM_LIMIT_BYTES)


def _full(shape):
    return pl.BlockSpec(shape, lambda *_: (0,) * len(shape))


def _rms(x, g):
    return x * lax.rsqrt(jnp.mean(x * x, axis=-1, keepdims=True) + NORM_EPS) * g


def _sigmoid(x):
    return 1.0 / (1.0 + jnp.exp(-x))


def _rope_slab(t, cos, sin_signed):
    lane = lax.broadcasted_iota(jnp.int32, t.shape, 1)
    swapped = jnp.where(lane % DA_HD < DA_HD // 2,
                        pltpu.roll(t, LANES - DA_HD // 2, 1),
                        pltpu.roll(t, DA_HD // 2, 1))
    return t * cos + swapped * sin_signed


def _in_proj_kernel(x_ref, g_ref, w_ref, cos_ref, sin_ref, q_ref, k_ref, v_ref, p_ref):
    xn = _rms(x_ref[...], g_ref[...])
    proj = jnp.dot(xn.astype(BF16), w_ref[...], preferred_element_type=F32)
    cos = cos_ref[...]
    sin = sin_ref[...]
    for s in range(DA_QK // LANES):
        lo = s * LANES
        q_ref[:, lo:lo + LANES] = _rope_slab(proj[:, lo:lo + LANES], cos, sin)
        k_ref[:, lo:lo + LANES] = _rope_slab(proj[:, DA_QK + lo:DA_QK + lo + LANES], cos, sin)
    v_ref[...] = proj[:, 2 * DA_QK:2 * DA_QK + DA_WIDTH]
    p_ref[...] = proj[:, 2 * DA_QK + DA_WIDTH:]


def _in_proj(x, g, w_bf, cos, sin, tm):
    n = x.shape[0]
    row = lambda w: pl.BlockSpec((tm, w), lambda i: (i, 0))
    return pl.pallas_call(
        _in_proj_kernel,
        grid=(n // tm,),
        in_specs=[row(D_MODEL), _full((1, D_MODEL)), _full((D_MODEL, IN_PROJ)), row(LANES), row(LANES)],
        out_specs=[row(DA_QK), row(DA_QK), row(DA_WIDTH), row(RW_PROJ)],
        out_shape=[jax.ShapeDtypeStruct((n, DA_QK), F32), jax.ShapeDtypeStruct((n, DA_QK), F32),
                   jax.ShapeDtypeStruct((n, DA_WIDTH), F32), jax.ShapeDtypeStruct((n, RW_PROJ), F32)],
        compiler_params=_cparams(("parallel",)),
        name="in_proj",
    )(x, g, w_bf, cos, sin)


def _rope_tables(pos):
    half = DA_HD // 2
    inv = ROPE_THETA ** (-jnp.arange(half, dtype=F32) / half)
    ang = pos.astype(F32)[:, None] * inv[None, :]
    cos, sin = jnp.cos(ang), jnp.sin(ang)
    cos = jnp.tile(cos, (1, LANES // half))
    sin = jnp.tile(jnp.concatenate([-sin, sin], axis=1), (1, LANES // DA_HD))
    return cos, sin


def _lambda(lq1, lk1, lq2, lk2):
    return (jnp.exp(jnp.sum(lq1 * lk1, axis=-1, keepdims=True))
            - jnp.exp(jnp.sum(lq2 * lk2, axis=-1, keepdims=True)) + LAM_INIT)


def _online(s, vb, m_ref, l_ref, acc_ref):
    m_old = m_ref[...]
    m_new = jnp.maximum(m_old, jnp.max(s, axis=-1, keepdims=True))
    alpha = jnp.exp(m_old - m_new)
    p = jnp.exp(s - m_new)
    l_ref[...] = alpha * l_ref[...] + jnp.sum(p, axis=-1, keepdims=True)
    acc_ref[...] = alpha * acc_ref[...] + jnp.dot(p.astype(BF16), vb, preferred_element_type=F32)
    m_ref[...] = m_new


def _attn_prompt_kernel(q_ref, k_ref, v_ref, lq1, lk1, lq2, lk2, g_ref, o_ref,
                        m1, l1, a1, m2, l2, a2, *, tq):
    qi = pl.program_id(2)
    ki = pl.program_id(3)

    @pl.when(ki == 0)
    def _():
        for m, l, a in ((m1, l1, a1), (m2, l2, a2)):
            m[...] = jnp.full(m.shape, NEG_INF, F32)
            l[...] = jnp.zeros(l.shape, F32)
            a[...] = jnp.zeros(a.shape, F32)

    @pl.when(ki <= qi)
    def _():
        q = q_ref[0] * (DA_HD ** -0.5)
        lane = lax.broadcasted_iota(jnp.int32, q.shape, 1)
        q1 = jnp.where(lane < DA_HD, q, 0.0).astype(BF16)
        q2 = jnp.where(lane < DA_HD, 0.0, q).astype(BF16)
        kb = k_ref[0].astype(BF16)
        vb = v_ref[0].astype(BF16)
        row = lax.broadcasted_iota(jnp.int32, (tq, tq), 0) + qi * tq
        col = lax.broadcasted_iota(jnp.int32, (tq, tq), 1) + ki * tq
        mask = col <= row
        dn = (((1,), (1,)), ((), ()))
        s1 = jnp.where(mask, lax.dot_general(q1, kb, dn, preferred_element_type=F32), NEG_INF)
        _online(s1, vb, m1, l1, a1)
        s2 = jnp.where(mask, lax.dot_general(q2, kb, dn, preferred_element_type=F32), NEG_INF)
        _online(s2, vb, m2, l2, a2)

    @pl.when(ki == qi)
    def _():
        lam = _lambda(lq1[...], lk1[...], lq2[...], lk2[...])
        o = a1[...] / l1[...] - lam * (a2[...] / l2[...])
        o_ref[0] = _rms(o, g_ref[...]) * (1.0 - LAM_INIT)


def _attn_prompt(q, k, v, lams, subln, tq):
    b, t, _ = q.shape
    nq = t // tq
    qspec = pl.BlockSpec((1, tq, DA_VD), lambda bi, h, qi, ki: (bi, qi, h))
    kspec = pl.BlockSpec((1, tq, DA_VD), lambda bi, h, qi, ki: (bi, jnp.minimum(ki, qi), h))
    vec = lambda w: _full((1, w))
    return pl.pallas_call(
        functools.partial(_attn_prompt_kernel, tq=tq),
        grid=(b, DA_HEADS, nq, nq),
        in_specs=[qspec, kspec, kspec, vec(DA_HD), vec(DA_HD), vec(DA_HD), vec(DA_HD), vec(DA_VD)],
        out_specs=qspec,
        out_shape=jax.ShapeDtypeStruct((b, t, DA_WIDTH), F32),
        scratch_shapes=[pltpu.VMEM((tq, 1), F32), pltpu.VMEM((tq, 1), F32), pltpu.VMEM((tq, DA_VD), F32),
                        pltpu.VMEM((tq, 1), F32), pltpu.VMEM((tq, 1), F32), pltpu.VMEM((tq, DA_VD), F32)],
        compiler_params=_cparams(("parallel", "parallel", "parallel", "arbitrary")),
        name="attn_prompt",
    )(q, k, v, *lams, subln)


def _attn_sample_kernel(pt_ref, q_ref, ck_ref, cv_ref, kn_ref, vn_ref, lq1, lk1, lq2, lk2, g_ref, o_ref,
                        qb_ref, m_ref, l_ref, acc_ref, *, n_pages, t_new):
    p = pl.program_id(1)
    n_rows = DA_HEADS * 2 * t_new

    @pl.when(p == 0)
    def _():
        q = q_ref[0] * (DA_HD ** -0.5)
        lane = lax.broadcasted_iota(jnp.int32, q.shape, 1)
        for hc in range(DA_HEADS * 2):
            sel = (lane >= hc * DA_HD) & (lane < (hc + 1) * DA_HD)
            qb_ref[hc * t_new:(hc + 1) * t_new, :] = jnp.where(sel, q, 0.0)
        m_ref[...] = jnp.full(m_ref.shape, NEG_INF, F32)
        l_ref[...] = jnp.zeros(l_ref.shape, F32)
        acc_ref[...] = jnp.zeros(acc_ref.shape, F32)

    @pl.when(p < n_pages)
    def _():
        dn = (((1,), (1,)), ((), ()))
        s = lax.dot_general(qb_ref[...].astype(BF16), ck_ref[0].astype(BF16), dn, preferred_element_type=F32)
        _online(s, cv_ref[0].astype(BF16), m_ref, l_ref, acc_ref)

    @pl.when(p == n_pages)
    def _():
        qb = qb_ref[...]
        kn = kn_ref[0]
        vn = vn_ref[0]
        t_of_row = lax.broadcasted_iota(jnp.int32, (n_rows, 1), 0) % t_new
        s_new = []
        for j in range(t_new):
            s_j = jnp.sum(qb * kn[j:j + 1, :], axis=-1, keepdims=True)
            s_new.append(jnp.where(t_of_row >= j, s_j, NEG_INF))
        m_old = m_ref[...]
        m_new = m_old
        for s_j in s_new:
            m_new = jnp.maximum(m_new, s_j)
        alpha = jnp.exp(m_old - m_new)
        l = alpha * l_ref[...]
        acc = alpha * acc_ref[...]
        for j, s_j in enumerate(s_new):
            p_j = jnp.exp(s_j - m_new)
            l = l + p_j
            acc = acc + p_j * vn[j:j + 1, :]
        o_all = acc / l
        lam = _lambda(lq1[...], lk1[...], lq2[...], lk2[...])
        for h in range(DA_HEADS):
            r1 = (2 * h) * t_new
            r2 = (2 * h + 1) * t_new
            lo = h * DA_VD
            o = o_all[r1:r1 + t_new, lo:lo + DA_VD] - lam * o_all[r2:r2 + t_new, lo:lo + DA_VD]
            o_ref[0, :, lo:lo + DA_VD] = _rms(o, g_ref[...]) * (1.0 - LAM_INIT)


def _attn_sample(q, k_new, v_new, cache_k, cache_v, page_table, lams, subln):
    bd, t_new, _ = q.shape
    n_pages = page_table.shape[1]
    n_rows = DA_HEADS * 2 * t_new
    tok = pl.BlockSpec((1, t_new, DA_WIDTH), lambda b, p, pt: (b, 0, 0))
    page = pl.BlockSpec((1, PAGE_SIZE, DA_WIDTH), lambda b, p, pt: (pt[b, jnp.minimum(p, n_pages - 1)], 0, 0))
    vec = lambda w: pl.BlockSpec((1, w), lambda b, p, pt: (0, 0))
    grid_spec = pltpu.PrefetchScalarGridSpec(
        num_scalar_prefetch=1,
        grid=(bd, n_pages + 1),
        in_specs=[tok, page, page, tok, tok, vec(DA_HD), vec(DA_HD), vec(DA_HD), vec(DA_HD), vec(DA_VD)],
        out_specs=tok,
        scratch_shapes=[pltpu.VMEM((n_rows, DA_WIDTH), F32), pltpu.VMEM((n_rows, 1), F32),
                        pltpu.VMEM((n_rows, 1), F32), pltpu.VMEM((n_rows, DA_WIDTH), F32)],
    )
    return pl.pallas_call(
        functools.partial(_attn_sample_kernel, n_pages=n_pages, t_new=t_new),
        grid_spec=grid_spec,
        out_shape=jax.ShapeDtypeStruct((bd, t_new, DA_WIDTH), F32),
        compiler_params=_cparams(("parallel", "arbitrary")),
        name="attn_sample",
    )(page_table, q, cache_k, cache_v, k_new, v_new, *lams, subln)


def _head_sum(x, seg):
    return jnp.dot(x, seg, preferred_element_type=F32, precision=HIGHEST)


def _rw_pre_kernel(p_ref, first_ref, mu_ref, wcat_ref, w0_ref, a0_ref, kk_ref, ka_ref, rk_ref, seg_ref,
                   r_out, w_out, k_out, v_out, kk_out, b_out, g_out, bonus_out):
    p = p_ref[...]
    gb, rows, _ = p.shape
    row = lax.broadcasted_iota(jnp.int32, p.shape, 1)
    p_prev = jnp.where(row == 0, first_ref[...], pltpu.roll(p, 1, 1))
    ps = (p + (p_prev - p) * mu_ref[...]).reshape(gb * rows, RW_PROJ)
    w_ = RW_WIDTH
    r, k, v = ps[:, :w_], ps[:, w_:2 * w_], ps[:, 2 * w_:3 * w_]
    x = ps[:, 3 * w_:]
    lane = lax.broadcasted_iota(jnp.int32, x.shape, 1)
    act = jnp.where(lane < RW_W_RANK, jnp.tanh(x), jnp.where(lane < RW_W_RANK + RW_A_RANK, x, _sigmoid(x)))
    lora = jnp.dot(act.astype(BF16), wcat_ref[...], preferred_element_type=F32)
    decay = jnp.exp(-math.exp(-0.5) * _sigmoid(w0_ref[...] + lora[:, :w_]))
    a = _sigmoid(a0_ref[...] + lora[:, w_:2 * w_])
    seg = seg_ref[...]
    kk = k * kk_ref[...]
    kk = kk / jnp.maximum(jnp.sqrt(_head_sum(kk * kk, seg)), 1e-12)
    k_mod = k * (1.0 + (a - 1.0) * ka_ref[...])
    r_out[...] = r
    w_out[...] = decay
    k_out[...] = k_mod
    v_out[...] = v
    kk_out[...] = kk
    b_out[...] = kk * a
    g_out[...] = lora[:, 2 * w_:]
    bonus_out[...] = _head_sum(r * k_mod * rk_ref[...], seg) * v


def _rw_pre(p3, first, mu, wcat, w0, a0, k_k, k_a, r_k, seg, gb):
    g, rows, _ = p3.shape
    n = g * rows
    vec = lambda w: _full((1, w))
    out = pl.BlockSpec((gb * rows, RW_WIDTH), lambda i: (i, 0))
    return pl.pallas_call(
        _rw_pre_kernel,
        grid=(g // gb,),
        in_specs=[pl.BlockSpec((gb, rows, RW_PROJ), lambda i: (i, 0, 0)),
                  pl.BlockSpec((gb, 1, RW_PROJ), lambda i: (i, 0, 0)),
                  vec(RW_PROJ), _full((RW_LORA, 3 * RW_WIDTH)), vec(RW_WIDTH), vec(RW_WIDTH),
                  vec(RW_WIDTH), vec(RW_WIDTH), vec(RW_WIDTH), _full((RW_WIDTH, RW_WIDTH))],
        out_specs=[out] * 8,
        out_shape=[jax.ShapeDtypeStruct((n, RW_WIDTH), F32)] * 8,
        compiler_params=_cparams(("parallel",)),
        name="rw_pre",
    )(p3, first, mu, wcat, w0, a0, k_k, k_a, r_k, seg)


def _rw_scan_kernel(r_ref, w_ref, k_ref, v_ref, kk_ref, b_ref, h0_ref, y_ref, h_ref, *, tc):
    @pl.when(pl.program_id(1) == 0)
    def _():
        h_ref[...] = h0_ref[...]

    def key_row(ref, t, j):
        return ref[0, t, pl.ds(j, 1), :]

    def step(t, carry):
        parts = [None] * 4
        for j in range(RW_HD):
            term = h_ref[0, j] * key_row(kk_ref, t, j)
            parts[j % 4] = term if parts[j % 4] is None else parts[j % 4] + term
        sa = -((parts[0] + parts[1]) + (parts[2] + parts[3]))
        v = v_ref[0, t]
        parts = [None] * 4
        for j in range(RW_HD):
            h = h_ref[0, j] * key_row(w_ref, t, j) + sa * key_row(b_ref, t, j) + v * key_row(k_ref, t, j)
            h_ref[0, j] = h
            term = h * key_row(r_ref, t, j)
            parts[j % 4] = term if parts[j % 4] is None else parts[j % 4] + term
        y_ref[0, t] = (parts[0] + parts[1]) + (parts[2] + parts[3])
        return carry

    lax.fori_loop(0, tc, step, 0)


def _rw_scan(r, w, k, v, kk, b, h0, tc):
    nb, t, rows = v.shape[0], v.shape[1], v.shape[2]
    key = pl.BlockSpec((1, tc, RW_HD, LANES), lambda i, c: (i, c, 0, 0))
    val = pl.BlockSpec((1, tc, rows, LANES), lambda i, c: (i, c, 0, 0))
    st = pl.BlockSpec((1, RW_HD, rows, LANES), lambda i, c: (i, 0, 0, 0))
    return pl.pallas_call(
        functools.partial(_rw_scan_kernel, tc=tc),
        grid=(nb, t // tc),
        in_specs=[key, key, key, val, key, key, st],
        out_specs=[val, st],
        out_shape=[jax.ShapeDtypeStruct((nb, t, rows, LANES), F32),
                   jax.ShapeDtypeStruct((nb, RW_HD, rows, LANES), F32)],
        compiler_params=_cparams(("parallel", "arbitrary")),
        name="rw_scan",
    )(r, w, k, v, kk, b, h0)


def _scan_split(n_states):
    slices = max(1, LANES // n_states)
    assert (n_states * slices) % LANES == 0 and RW_HD % slices == 0 and (RW_HD // slices) % SUBLANES == 0, n_states
    return slices, LANES // slices


def _key_cols(x, slices, per_block):
    b, t, h, j = x.shape
    x = x.transpose(1, 3, 0, 2).reshape(t, j, (b * h) // per_block, 1, per_block)
    x = jnp.broadcast_to(x, (t, j, (b * h) // per_block, slices, per_block))
    return x.transpose(2, 0, 1, 3, 4).reshape((b * h) // per_block, t, j, LANES)


def _value_cols(x, slices, per_block):
    b, t, h, i = x.shape
    rows = i // slices
    x = x.reshape(b, t, h, slices, rows).transpose(1, 4, 3, 0, 2).reshape(t, rows, slices, (b * h) // per_block, per_block)
    return x.transpose(3, 0, 1, 2, 4).reshape((b * h) // per_block, t, rows, LANES)


def _value_cols_inv(y, b, h, slices, per_block):
    nb, t, rows, _ = y.shape
    y = y.reshape(nb, t, rows, slices, per_block).transpose(1, 2, 3, 0, 4).reshape(t, rows, slices, b, h)
    return y.transpose(3, 0, 4, 2, 1).reshape(b, t, h, slices * rows)


def _state_cols(s, slices, per_block):
    b, h, i, j = s.shape
    rows = i // slices
    s = s.reshape(b, h, slices, rows, j).transpose(4, 3, 2, 0, 1).reshape(j, rows, slices, (b * h) // per_block, per_block)
    return s.transpose(3, 0, 1, 2, 4).reshape((b * h) // per_block, j, rows, LANES)


def _state_cols_inv(hs, b, h, slices, per_block):
    nb, j, rows, _ = hs.shape
    hs = hs.reshape(nb, j, rows, slices, per_block).transpose(1, 2, 3, 0, 4).reshape(j, rows, slices, b, h)
    return hs.transpose(3, 4, 2, 1, 0).reshape(b, h, slices * rows, j)


def _out_proj_kernel(x_ref, oda_ref, y_ref, bonus_ref, g_ref, lnw_ref, lnb_ref, seg_ref, wo_ref, fn_ref, wq_ref,
                     h_ref, hn_ref, qp_ref):
    seg = seg_ref[...]
    y = y_ref[...]
    d = y - _head_sum(y, seg) * (1.0 / RW_HD)
    var = _head_sum(d * d, seg) * (1.0 / RW_HD)
    yn = d * lax.rsqrt(var + RW_GN_EPS) * lnw_ref[...] + lnb_ref[...]
    o_rw = (yn + bonus_ref[...]) * g_ref[...]
    h = (x_ref[...]
         + jnp.dot(oda_ref[...].astype(BF16), wo_ref[:DA_WIDTH, :], preferred_element_type=F32)
         + jnp.dot(o_rw.astype(BF16), wo_ref[DA_WIDTH:, :], preferred_element_type=F32))
    hn = _rms(h, fn_ref[...])
    h_ref[...] = h
    hn_ref[...] = hn
    qp_ref[...] = jnp.dot(hn.astype(BF16), wq_ref[...], preferred_element_type=F32)


def _out_proj(x, o_da, y, bonus, g, ln_w, ln_b, seg, w_out_bf, ffn_norm, wq_bf, tm):
    n = x.shape[0]
    row = lambda w: pl.BlockSpec((tm, w), lambda i: (i, 0))
    vec = lambda w: _full((1, w))
    return pl.pallas_call(
        _out_proj_kernel,
        grid=(n // tm,),
        in_specs=[row(D_MODEL), row(DA_WIDTH), row(RW_WIDTH), row(RW_WIDTH), row(RW_WIDTH),
                  vec(RW_WIDTH), vec(RW_WIDTH), _full((RW_WIDTH, RW_WIDTH)),
                  _full((DA_WIDTH + RW_WIDTH, D_MODEL)), vec(D_MODEL), _full((D_MODEL, PEER_QW))],
        out_specs=[row(D_MODEL), row(D_MODEL), row(PEER_QW)],
        out_shape=[jax.ShapeDtypeStruct((n, D_MODEL), F32), jax.ShapeDtypeStruct((n, D_MODEL), F32),
                   jax.ShapeDtypeStruct((n, PEER_QW), F32)],
        compiler_params=_cparams(("parallel",)),
        name="out_proj",
    )(x, o_da, y, bonus, g, ln_w, ln_b, seg, w_out_bf, ffn_norm, wq_bf)


def _top_k_rows(s, payload, k):
    n = s.shape[0]
    row = lax.broadcasted_iota(jnp.int32, s.shape, 0).astype(F32)
    vals, pays = [], []
    for _ in range(k):
        m = jnp.max(s, axis=0, keepdims=True)
        first = jnp.min(jnp.where(s == m, row, float(n)), axis=0, keepdims=True)
        hit = row == first
        vals.append(m)
        pays.append(jnp.max(jnp.where(hit, payload, -1.0), axis=0, keepdims=True))
        s = jnp.where(hit, -jnp.inf, s)
    return jnp.concatenate(vals, axis=0), jnp.concatenate(pays, axis=0)


def _peer_topk_kernel(qp_ref, keys_ref, idx_ref, gate_ref, *, tm):
    key_row = lax.broadcasted_iota(jnp.int32, (PEER_NKEYS, tm), 0).astype(F32)
    dn = (((1,), (1,)), ((), ()))
    idx_rows, gate_rows = [], []
    for h in range(PEER_HEADS):
        sv, si = [], []
        for c in range(2):
            hc = 2 * h + c
            q = qp_ref[:, hc * PEER_KD:(hc + 1) * PEER_KD].astype(BF16)
            s = lax.dot_general(keys_ref[hc], q, dn, preferred_element_type=F32)
            v_, i_ = _top_k_rows(s, key_row, PEER_TOPK)
            sv.append(v_)
            si.append(i_)
        cand = jnp.concatenate([sv[0][a:a + 1, :] + sv[1] for a in range(PEER_TOPK)], axis=0)
        cidx = jnp.concatenate([si[0][a:a + 1, :] * PEER_NKEYS + si[1] for a in range(PEER_TOPK)], axis=0)
        bv, eidx = _top_k_rows(cand, cidx, PEER_TOPK)
        e = jnp.exp(bv - bv[0:1, :])
        gate_rows.append(e / jnp.sum(e, axis=0, keepdims=True))
        idx_rows.append(eidx)
    idx_ref[...] = jnp.concatenate(idx_rows, axis=0).T.astype(jnp.int32)
    gate_ref[...] = jnp.concatenate(gate_rows, axis=0).T


def _peer_topk(qp, keys_bf, tm):
    n = qp.shape[0]
    return pl.pallas_call(
        functools.partial(_peer_topk_kernel, tm=tm),
        grid=(n // tm,),
        in_specs=[pl.BlockSpec((tm, PEER_QW), lambda i: (i, 0)),
                  _full((PEER_HEADS * 2, PEER_NKEYS, PEER_KD))],
        out_specs=[pl.BlockSpec((tm, PEER_PICKS), lambda i: (i, 0))] * 2,
        out_shape=[jax.ShapeDtypeStruct((n, PEER_PICKS), jnp.int32),
                   jax.ShapeDtypeStruct((n, PEER_PICKS), F32)],
        compiler_params=_cparams(("parallel",)),
        name="peer_topk",
    )(qp, keys_bf)


GATE_PITCH = PEER_NKEYS + SUBLANES


def _peer_expert_kernel(idx_ref, gate_ref, hn_ref, h_ref, fn_ref, u_ref, v_ref, out_ref, g_ref, acc_ref, *, tm, te):
    c = pl.program_id(1)
    tiles = te // PEER_NKEYS
    dn = (((1,), (1,)), ((), ()))

    @pl.when(c == 0)
    def _():
        sub = lax.broadcasted_iota(jnp.int32, (PEER_NKEYS, PEER_PICKS), 0)

        def token(t, carry):
            e = idx_ref[pl.ds(t, 1), :]
            g = gate_ref[pl.ds(t, 1), :]
            g_hi = g.astype(BF16).astype(F32)
            hit1 = sub == lax.shift_right_logical(e, 7)
            hit2 = sub == (e & (PEER_NKEYS - 1))
            m1 = jnp.where(hit1, 1.0, 0.0).astype(BF16)
            m2_hi = jnp.where(hit2, g_hi, 0.0).astype(BF16)
            m2_lo = jnp.where(hit2, g - g_hi, 0.0).astype(BF16)
            dense = lax.dot_general(jnp.concatenate([m1, m1], axis=1), jnp.concatenate([m2_hi, m2_lo], axis=1), dn,
                                    preferred_element_type=F32)
            g_ref[pl.ds(pl.multiple_of(t * GATE_PITCH, SUBLANES), PEER_NKEYS), :] = dense
            return carry

        lax.fori_loop(0, tm, token, 0)
        acc_ref[...] = jnp.zeros(acc_ref.shape, F32)

    lin = lax.dot_general(hn_ref[...].astype(BF16), u_ref[...], dn, preferred_element_type=F32)
    act = jax.nn.gelu(lin)
    parts = []
    for j in range(tiles):
        gates = g_ref[pl.ds(c * tiles + j, tm, stride=GATE_PITCH), :]
        parts.append((act[:, j * PEER_NKEYS:(j + 1) * PEER_NKEYS] * gates).astype(BF16))
    acc_ref[...] += jnp.dot(jnp.concatenate(parts, axis=1), v_ref[...], preferred_element_type=F32)

    @pl.when(c == pl.num_programs(1) - 1)
    def _():
        out_ref[...] = _rms(h_ref[...] + acc_ref[...], fn_ref[...])


def _peer_expert(idx, gate, hn, h, final_norm, u_bf, v_bf, tm, te):
    n = hn.shape[0]
    n_exp = u_bf.shape[0]
    row = lambda w: pl.BlockSpec((tm, w), lambda i, c: (i, 0))
    tab = pl.BlockSpec((te, D_MODEL), lambda i, c: (c, 0))
    return pl.pallas_call(
        functools.partial(_peer_expert_kernel, tm=tm, te=te),
        grid=(n // tm, n_exp // te),
        in_specs=[row(PEER_PICKS), row(PEER_PICKS), row(D_MODEL), row(D_MODEL),
                  pl.BlockSpec((1, D_MODEL), lambda i, c: (0, 0)), tab, tab],
        out_specs=row(D_MODEL),
        out_shape=jax.ShapeDtypeStruct((n, D_MODEL), F32),
        scratch_shapes=[pltpu.VMEM((tm * GATE_PITCH, PEER_NKEYS), F32), pltpu.VMEM((tm, D_MODEL), F32)],
        compiler_params=_cparams(("parallel", "arbitrary")),
        name="peer_expert",
    )(idx, gate, hn, h, final_norm, u_bf, v_bf)


def _row_tile(n, target):
    tm = min(n, target)
    while n % tm:
        tm //= 2
    return tm


def _group(x, pos, attend, wkv0, shift0, wts):
    b, t, _ = x.shape
    n = b * t
    x2 = x.reshape(n, D_MODEL)
    tm = _row_tile(n, 256)
    cos, sin = _rope_tables(jnp.tile(pos, b))
    q, k, v, p_rw = _in_proj(x2, wts["attn_norm"], wts["w_in"], cos, sin, tm)

    o_da = attend(q.reshape(b, t, DA_QK), k.reshape(b, t, DA_QK), v.reshape(b, t, DA_WIDTH))

    rows = _row_tile(t, 256)
    p3 = p_rw.reshape(n // rows, rows, RW_PROJ)
    p_bt = p_rw.reshape(b, t, RW_PROJ)
    prev_rows = p_bt[:, rows - 1::rows][:, :t // rows - 1]
    first = jnp.concatenate([shift0[:, None], prev_rows], axis=1).reshape(n // rows, 1, RW_PROJ)
    gb = _row_tile(n // rows, max(1, 256 // rows))
    r, w, k_mod, v_rw, kk, bb_, g, bonus = _rw_pre(
        p3, first, wts["rw_mu"], wts["rw_wcat"], wts["rw_w0"], wts["rw_a0"],
        wts["rw_k_k"], wts["rw_k_a"], wts["rw_r_k"], wts["seg"], gb)
    slices, per_block = _scan_split(b * RW_HEADS)
    heads = lambda a: a.reshape(b, t, RW_HEADS, RW_HD)
    key = lambda a: _key_cols(heads(a), slices, per_block)
    y_cols, h_cols = _rw_scan(key(r), key(w), key(k_mod), _value_cols(heads(v_rw), slices, per_block),
                              key(kk), key(bb_), _state_cols(wkv0, slices, per_block), tc=_row_tile(t, 64))
    y = _value_cols_inv(y_cols, b, RW_HEADS, slices, per_block)
    wkv = _state_cols_inv(h_cols, b, RW_HEADS, slices, per_block)

    h, hn, qp = _out_proj(x2, o_da.reshape(n, DA_WIDTH), y.reshape(n, RW_WIDTH), bonus, g,
                          wts["rw_ln_w"], wts["rw_ln_b"], wts["seg"], wts["w_out"], wts["ffn_norm"],
                          wts["peer_wq"], tm)
    idx, gate = _peer_topk(qp, wts["peer_keys"], tm)
    out = _peer_expert(idx, gate, hn, h, wts["final_norm"], wts["peer_u"], wts["peer_v"], tm, te=1024)
    return (out.reshape(b, t, D_MODEL), k.reshape(b, t, DA_HEADS, 2 * DA_HD), v.reshape(b, t, DA_HEADS, DA_VD),
            wkv, p_bt[:, -1])


def kernel(x_prompt, x_sample, cache_k, cache_v, state_wkv, state_shift, page_table, attn_norm, w_in, w_out,
           da_lambda_q1, da_lambda_k1, da_lambda_q2, da_lambda_k2, da_subln, rw_mu, rw_w0, rw_w_up, rw_a0,
           rw_a_up, rw_g_up, rw_k_k, rw_k_a, rw_r_k, rw_ln_w, rw_ln_b, ffn_norm, peer_wq, peer_sub_keys,
           peer_u, peer_v, final_norm):
    assert w_in.shape[0] == 1, "single-layer trunk"
    b, t_p, _ = x_prompt.shape
    bd, t_s, _ = x_sample.shape
    past = page_table.shape[1] * PAGE_SIZE
    vec = lambda a: a.reshape(1, -1).astype(F32)

    wcat = jnp.zeros((RW_LORA, 3 * RW_WIDTH), F32)
    wcat = wcat.at[:RW_W_RANK, :RW_WIDTH].set(rw_w_up[0])
    wcat = wcat.at[RW_W_RANK:RW_W_RANK + RW_A_RANK, RW_WIDTH:2 * RW_WIDTH].set(rw_a_up[0])
    wcat = wcat.at[RW_W_RANK + RW_A_RANK:, 2 * RW_WIDTH:].set(rw_g_up[0])
    lane_head = jnp.arange(RW_WIDTH) // RW_HD
    wts = dict(
        attn_norm=vec(attn_norm[0]), w_in=w_in[0].astype(BF16), w_out=w_out[0].astype(BF16),
        rw_mu=vec(rw_mu[0]), rw_wcat=wcat.astype(BF16), rw_w0=vec(rw_w0[0]), rw_a0=vec(rw_a0[0]),
        rw_k_k=vec(rw_k_k[0]), rw_k_a=vec(rw_k_a[0]), rw_r_k=vec(rw_r_k[0]),
        rw_ln_w=vec(rw_ln_w[0]), rw_ln_b=vec(rw_ln_b[0]),
        seg=(lane_head[:, None] == lane_head[None, :]).astype(F32),
        ffn_norm=vec(ffn_norm[0]), peer_wq=peer_wq[0].astype(BF16),
        peer_keys=peer_sub_keys[0].reshape(PEER_HEADS * 2, PEER_NKEYS, PEER_KD).astype(BF16),
        peer_u=peer_u.reshape(-1, D_MODEL).astype(BF16), peer_v=peer_v.reshape(-1, D_MODEL).astype(BF16),
        final_norm=vec(final_norm),
    )
    lams = (vec(da_lambda_q1[0]), vec(da_lambda_k1[0]), vec(da_lambda_q2[0]), vec(da_lambda_k2[0]))
    subln = vec(da_subln[0])

    att_p = lambda q, k, v: _attn_prompt(q, k, v, lams, subln, _row_tile(t_p, 512))
    ck = cache_k[0].reshape(-1, PAGE_SIZE, DA_QK)
    cv = cache_v[0].reshape(-1, PAGE_SIZE, DA_WIDTH)
    att_s = lambda q, k, v: _attn_sample(q, k, v, ck, cv, page_table, lams, subln)

    pos_p = jnp.arange(t_p, dtype=jnp.int32)
    pos_s = past + jnp.arange(t_s, dtype=jnp.int32)
    yp, kp, vp, wp, sp = _group(x_prompt, pos_p, att_p, jnp.zeros((b, RW_HEADS, RW_HD, RW_HD), F32),
                                jnp.zeros((b, RW_PROJ), F32), wts)
    ys, ks, vs, ws, ss = _group(x_sample, pos_s, att_s, state_wkv[0], state_shift[0], wts)
    return (yp, ys, kp[None], vp[None], wp[None], sp[None], ks[None], vs[None], ws[None], ss[None])
```

```python
import functools
import math

import jax
import jax.numpy as jnp
from jax import lax
from jax.experimental import pallas as pl
from jax.experimental.pallas import tpu as pltpu

F32 = jnp.float32
BF16 = jnp.bfloat16

D_MODEL = 1024
PAGE_SIZE = 128
DA_HEADS = 4
DA_HD = 64
DA_VD = 2 * DA_HD
DA_WIDTH = DA_HEADS * DA_VD
DA_QK = DA_HEADS * 2 * DA_HD
RW_HEADS = 8
RW_HD = 64
RW_WIDTH = RW_HEADS * RW_HD
RW_W_RANK = 64
RW_A_RANK = 64
RW_G_RANK = 128
RW_LORA = RW_W_RANK + RW_A_RANK + RW_G_RANK
RW_PROJ = 3 * RW_WIDTH + RW_LORA
RW_GN_EPS = 64e-5
IN_PROJ = 2 * DA_QK + DA_WIDTH + RW_PROJ
PEER_HEADS = 8
PEER_NKEYS = 128
PEER_KD = 128
PEER_TOPK = 16
PEER_PICKS = PEER_HEADS * PEER_TOPK
PEER_QW = PEER_HEADS * 2 * PEER_KD
ROPE_THETA = 10000.0
NORM_EPS = 1e-6
NEG_INF = -1e30
LAM_INIT = 0.8 - 0.6 * math.exp(-0.3 * 0)

LANES = 128
SUBLANES = 8
VMEM_LIMIT_BYTES = 48 * 1024 * 1024

HIGHEST = lax.Precision.HIGHEST


def _cparams(sem):
    return pltpu.CompilerParams(dimension_semantics=sem, vmem_limit_bytes=VMEM_LIMIT_BYTES)


def _full(shape):
    return pl.BlockSpec(shape, lambda *_: (0,) * len(shape))


def _rms(x, g):
    return x * lax.rsqrt(jnp.mean(x * x, axis=-1, keepdims=True) + NORM_EPS) * g


def _sigmoid(x):
    return 1.0 / (1.0 + jnp.exp(-x))


def _rope_slab(t, cos, sin_signed):
    lane = lax.broadcasted_iota(jnp.int32, t.shape, 1)
    swapped = jnp.where(lane % DA_HD < DA_HD // 2,
                        pltpu.roll(t, LANES - DA_HD // 2, 1),
                        pltpu.roll(t, DA_HD // 2, 1))
    return t * cos + swapped * sin_signed


def _in_proj_kernel(x_ref, g_ref, w_ref, cos_ref, sin_ref, q_ref, k_ref, v_ref, p_ref):
    xn = _rms(x_ref[...], g_ref[...])
    proj = jnp.dot(xn.astype(BF16), w_ref[...], preferred_element_type=F32)
    cos = cos_ref[...]
    sin = sin_ref[...]
    for s in range(DA_QK // LANES):
        lo = s * LANES
        q_ref[:, lo:lo + LANES] = _rope_slab(proj[:, lo:lo + LANES], cos, sin)
        k_ref[:, lo:lo + LANES] = _rope_slab(proj[:, DA_QK + lo:DA_QK + lo + LANES], cos, sin)
    v_ref[...] = proj[:, 2 * DA_QK:2 * DA_QK + DA_WIDTH]
    p_ref[...] = proj[:, 2 * DA_QK + DA_WIDTH:]


def _in_proj(x, g, w_bf, cos, sin, tm):
    n = x.shape[0]
    row = lambda w: pl.BlockSpec((tm, w), lambda i: (i, 0))
    return pl.pallas_call(
        _in_proj_kernel,
        grid=(n // tm,),
        in_specs=[row(D_MODEL), _full((1, D_MODEL)), _full((D_MODEL, IN_PROJ)), row(LANES), row(LANES)],
        out_specs=[row(DA_QK), row(DA_QK), row(DA_WIDTH), row(RW_PROJ)],
        out_shape=[jax.ShapeDtypeStruct((n, DA_QK), F32), jax.ShapeDtypeStruct((n, DA_QK), F32),
                   jax.ShapeDtypeStruct((n, DA_WIDTH), F32), jax.ShapeDtypeStruct((n, RW_PROJ), F32)],
        compiler_params=_cparams(("parallel",)),
        name="in_proj",
    )(x, g, w_bf, cos, sin)


def _rope_tables(pos):
    half = DA_HD // 2
    inv = ROPE_THETA ** (-jnp.arange(half, dtype=F32) / half)
    ang = pos.astype(F32)[:, None] * inv[None, :]
    cos, sin = jnp.cos(ang), jnp.sin(ang)
    cos = jnp.tile(cos, (1, LANES // half))
    sin = jnp.tile(jnp.concatenate([-sin, sin], axis=1), (1, LANES // DA_HD))
    return cos, sin


def _lambda(lq1, lk1, lq2, lk2):
    return (jnp.exp(jnp.sum(lq1 * lk1, axis=-1, keepdims=True))
            - jnp.exp(jnp.sum(lq2 * lk2, axis=-1, keepdims=True)) + LAM_INIT)


def _attn_prompt_kernel(q_ref, k_ref, v_ref, lq1, lk1, lq2, lk2, g_ref, o_ref,
                        m1, l1, a1, m2, l2, a2, *, tq):
    qi = pl.program_id(2)
    ki = pl.program_id(3)

    @pl.when(ki == 0)
    def _():
        for m, l, a in ((m1, l1, a1), (m2, l2, a2)):
            m[...] = jnp.full(m.shape, NEG_INF, F32)
            l[...] = jnp.zeros(l.shape, F32)
            a[...] = jnp.zeros(a.shape, F32)

    @pl.when(ki <= qi)
    def _():
        q = q_ref[0] * (DA_HD ** -0.5)
        lane = lax.broadcasted_iota(jnp.int32, q.shape, 1)
        q1 = jnp.where(lane < DA_HD, q, 0.0).astype(BF16)
        q2 = jnp.where(lane < DA_HD, 0.0, q).astype(BF16)
        kb = k_ref[0].astype(BF16)
        vb = v_ref[0].astype(BF16)
        row = lax.broadcasted_iota(jnp.int32, (tq, tq), 0) + qi * tq
        col = lax.broadcasted_iota(jnp.int32, (tq, tq), 1) + ki * tq
        mask = col <= row
        dn = (((1,), (1,)), ((), ()))
        state = [(m[...], l[...], a[...]) for m, l, a in ((m1, l1, a1), (m2, l2, a2))]
        new = []
        for qm, (m_old, l_old, a_old) in zip((q1, q2), state):
            s = jnp.where(mask, lax.dot_general(qm, kb, dn, preferred_element_type=F32), NEG_INF)
            m_new = jnp.maximum(m_old, jnp.max(s, axis=-1, keepdims=True))
            alpha = jnp.exp(m_old - m_new)
            p = jnp.exp(s - m_new)
            new.append((m_new, alpha * l_old + jnp.sum(p, axis=-1, keepdims=True),
                        alpha * a_old + jnp.dot(p.astype(BF16), vb, preferred_element_type=F32)))
        for (m, l, a), (m_new, l_new, a_new) in zip(((m1, l1, a1), (m2, l2, a2)), new):
            m[...] = m_new
            l[...] = l_new
            a[...] = a_new

    @pl.when(ki == qi)
    def _():
        lam = _lambda(lq1[...], lk1[...], lq2[...], lk2[...])
        o = a1[...] / l1[...] - lam * (a2[...] / l2[...])
        o_ref[0] = _rms(o, g_ref[...]) * (1.0 - LAM_INIT)


def _attn_prompt(q, k, v, lams, subln, tq):
    b, t, _ = q.shape
    nq = t // tq
    qspec = pl.BlockSpec((1, tq, DA_VD), lambda bi, h, qi, ki: (bi, qi, h))
    kspec = pl.BlockSpec((1, tq, DA_VD), lambda bi, h, qi, ki: (bi, jnp.minimum(ki, qi), h))
    vec = lambda w: _full((1, w))
    return pl.pallas_call(
        functools.partial(_attn_prompt_kernel, tq=tq),
        grid=(b, DA_HEADS, nq, nq),
        in_specs=[qspec, kspec, kspec, vec(DA_HD), vec(DA_HD), vec(DA_HD), vec(DA_HD), vec(DA_VD)],
        out_specs=qspec,
        out_shape=jax.ShapeDtypeStruct((b, t, DA_WIDTH), F32),
        scratch_shapes=[pltpu.VMEM((tq, 1), F32), pltpu.VMEM((tq, 1), F32), pltpu.VMEM((tq, DA_VD), F32),
                        pltpu.VMEM((tq, 1), F32), pltpu.VMEM((tq, 1), F32), pltpu.VMEM((tq, DA_VD), F32)],
        compiler_params=_cparams(("parallel", "parallel", "parallel", "arbitrary")),
        name="attn_prompt",
    )(q, k, v, *lams, subln)


def _attn_sample_kernel(pt_ref, q_ref, *refs, n_steps, pps, t_new):
    ck_refs, cv_refs = refs[:pps], refs[pps:2 * pps]
    kn_ref, vn_ref, lq1, lk1, lq2, lk2, g_ref, o_ref, qb_ref, m_ref, l_ref, acc_ref = refs[2 * pps:]
    p = pl.program_id(1)
    dn = (((1,), (1,)), ((), ()))

    @pl.when(p == 0)
    def _():
        q = q_ref[0] * (DA_HD ** -0.5)
        lane = lax.broadcasted_iota(jnp.int32, (t_new, DA_VD), 1)
        for h in range(DA_HEADS):
            slab = q[:, h * DA_VD:(h + 1) * DA_VD]
            qb_ref[h, 0:t_new, :] = jnp.where(lane < DA_HD, slab, 0.0)
            qb_ref[h, t_new:2 * t_new, :] = jnp.where(lane < DA_HD, 0.0, slab)
        m_ref[...] = jnp.full(m_ref.shape, NEG_INF, F32)
        l_ref[...] = jnp.zeros(l_ref.shape, F32)
        acc_ref[...] = jnp.zeros(acc_ref.shape, F32)

    @pl.when(p < n_steps)
    def _():
        head_rows = [pl.ds(h, PAGE_SIZE, stride=DA_HEADS) for h in range(DA_HEADS)]
        m_old = [m_ref[h] for h in range(DA_HEADS)]
        l_old = [l_ref[h] for h in range(DA_HEADS)]
        acc_old = [acc_ref[h] for h in range(DA_HEADS)]
        qb = [qb_ref[h].astype(BF16) for h in range(DA_HEADS)]
        s = [jnp.concatenate([lax.dot_general(qb[h], ck[head_rows[h], :].astype(BF16), dn, preferred_element_type=F32)
                              for ck in ck_refs], axis=1) for h in range(DA_HEADS)]
        m_new = [jnp.maximum(m_old[h], jnp.max(s[h], axis=-1, keepdims=True)) for h in range(DA_HEADS)]
        alpha = [jnp.exp(m_old[h] - m_new[h]) for h in range(DA_HEADS)]
        pr = [jnp.exp(s[h] - m_new[h]) for h in range(DA_HEADS)]
        l_new = [alpha[h] * l_old[h] + jnp.sum(pr[h], axis=-1, keepdims=True) for h in range(DA_HEADS)]
        acc_new = []
        for h in range(DA_HEADS):
            pv = None
            for i, cv in enumerate(cv_refs):
                term = jnp.dot(pr[h][:, i * PAGE_SIZE:(i + 1) * PAGE_SIZE].astype(BF16),
                               cv[head_rows[h], :].astype(BF16), preferred_element_type=F32)
                pv = term if pv is None else pv + term
            acc_new.append(alpha[h] * acc_old[h] + pv)
        for h in range(DA_HEADS):
            m_ref[h] = m_new[h]
            l_ref[h] = l_new[h]
            acc_ref[h] = acc_new[h]

    @pl.when(p == n_steps)
    def _():
        lam = _lambda(lq1[...], lk1[...], lq2[...], lk2[...])
        t_of_row = lax.broadcasted_iota(jnp.int32, (2 * t_new, 1), 0) % t_new
        for h in range(DA_HEADS):
            qb = qb_ref[h]
            kn = kn_ref[0, :, h * DA_VD:(h + 1) * DA_VD]
            vn = vn_ref[0, :, h * DA_VD:(h + 1) * DA_VD]
            s_new = [jnp.where(t_of_row >= j, jnp.sum(qb * kn[j:j + 1, :], axis=-1, keepdims=True), NEG_INF)
                     for j in range(t_new)]
            m_old = m_ref[h]
            m_new = m_old
            for s_j in s_new:
                m_new = jnp.maximum(m_new, s_j)
            alpha = jnp.exp(m_old - m_new)
            l = alpha * l_ref[h]
            acc = alpha * acc_ref[h]
            for j, s_j in enumerate(s_new):
                p_j = jnp.exp(s_j - m_new)
                l = l + p_j
                acc = acc + p_j * vn[j:j + 1, :]
            o_all = acc / l
            o = o_all[0:t_new] - lam * o_all[t_new:2 * t_new]
            o_ref[0, :, h * DA_VD:(h + 1) * DA_VD] = _rms(o, g_ref[...]) * (1.0 - LAM_INIT)


def _attn_sample(q, k_new, v_new, cache_k, cache_v, page_table, lams, subln):
    bd, t_new, _ = q.shape
    n_pages = page_table.shape[1]
    pps = 4 if n_pages % 4 == 0 else 1
    n_steps = n_pages // pps
    page_rows = PAGE_SIZE * DA_HEADS
    tok = pl.BlockSpec((1, t_new, DA_WIDTH), lambda b, p, pt: (b, 0, 0))

    def page(i):
        return pl.BlockSpec((page_rows, DA_VD),
                            lambda b, p, pt: (pt[b, jnp.minimum(p, n_steps - 1) * pps + i], 0))

    pages = [page(i) for i in range(pps)]
    vec = lambda w: pl.BlockSpec((1, w), lambda b, p, pt: (0, 0))
    rows = 2 * t_new
    grid_spec = pltpu.PrefetchScalarGridSpec(
        num_scalar_prefetch=1,
        grid=(bd, n_steps + 1),
        in_specs=[tok] + pages + pages + [tok, tok, vec(DA_HD), vec(DA_HD), vec(DA_HD), vec(DA_HD), vec(DA_VD)],
        out_specs=tok,
        scratch_shapes=[pltpu.VMEM((DA_HEADS, rows, DA_VD), F32), pltpu.VMEM((DA_HEADS, rows, 1), F32),
                        pltpu.VMEM((DA_HEADS, rows, 1), F32), pltpu.VMEM((DA_HEADS, rows, DA_VD), F32)],
    )
    return pl.pallas_call(
        functools.partial(_attn_sample_kernel, n_steps=n_steps, pps=pps, t_new=t_new),
        grid_spec=grid_spec,
        out_shape=jax.ShapeDtypeStruct((bd, t_new, DA_WIDTH), F32),
        compiler_params=_cparams(("parallel", "arbitrary")),
        name="attn_sample",
    )(page_table, q, *([cache_k] * pps), *([cache_v] * pps), k_new, v_new, *lams, subln)


def _head_sum(x, seg):
    return jnp.dot(x, seg, preferred_element_type=F32, precision=HIGHEST)


def _rw_pre_kernel(p_ref, first_ref, mu_ref, wcat_ref, w0_ref, a0_ref, kk_ref, ka_ref, rk_ref, seg_ref,
                   r_out, w_out, k_out, v_out, kk_out, b_out, g_out, bonus_out):
    p = p_ref[...]
    gb, rows, _ = p.shape
    row = lax.broadcasted_iota(jnp.int32, p.shape, 1)
    p_prev = jnp.where(row == 0, first_ref[...], pltpu.roll(p, 1, 1))
    ps = (p + (p_prev - p) * mu_ref[...]).reshape(gb * rows, RW_PROJ)
    w_ = RW_WIDTH
    r, k, v = ps[:, :w_], ps[:, w_:2 * w_], ps[:, 2 * w_:3 * w_]
    x = ps[:, 3 * w_:]
    lane = lax.broadcasted_iota(jnp.int32, x.shape, 1)
    act = jnp.where(lane < RW_W_RANK, jnp.tanh(x), jnp.where(lane < RW_W_RANK + RW_A_RANK, x, _sigmoid(x)))
    lora = jnp.dot(act.astype(BF16), wcat_ref[...], preferred_element_type=F32)
    decay = jnp.exp(-math.exp(-0.5) * _sigmoid(w0_ref[...] + lora[:, :w_]))
    a = _sigmoid(a0_ref[...] + lora[:, w_:2 * w_])
    seg = seg_ref[...]
    kk = k * kk_ref[...]
    kk = kk / jnp.maximum(jnp.sqrt(_head_sum(kk * kk, seg)), 1e-12)
    k_mod = k * (1.0 + (a - 1.0) * ka_ref[...])
    r_out[...] = r
    w_out[...] = decay
    k_out[...] = k_mod
    v_out[...] = v
    kk_out[...] = kk
    b_out[...] = kk * a
    g_out[...] = lora[:, 2 * w_:]
    bonus_out[...] = _head_sum(r * k_mod * rk_ref[...], seg) * v


def _rw_pre(p3, first, mu, wcat, w0, a0, k_k, k_a, r_k, seg, gb):
    g, rows, _ = p3.shape
    n = g * rows
    vec = lambda w: _full((1, w))
    out = pl.BlockSpec((gb * rows, RW_WIDTH), lambda i: (i, 0))
    return pl.pallas_call(
        _rw_pre_kernel,
        grid=(g // gb,),
        in_specs=[pl.BlockSpec((gb, rows, RW_PROJ), lambda i: (i, 0, 0)),
                  pl.BlockSpec((gb, 1, RW_PROJ), lambda i: (i, 0, 0)),
                  vec(RW_PROJ), _full((RW_LORA, 3 * RW_WIDTH)), vec(RW_WIDTH), vec(RW_WIDTH),
                  vec(RW_WIDTH), vec(RW_WIDTH), vec(RW_WIDTH), _full((RW_WIDTH, RW_WIDTH))],
        out_specs=[out] * 8,
        out_shape=[jax.ShapeDtypeStruct((n, RW_WIDTH), F32)] * 8,
        compiler_params=_cparams(("parallel",)),
        name="rw_pre",
    )(p3, first, mu, wcat, w0, a0, k_k, k_a, r_k, seg)


def _rw_scan_kernel(r_ref, w_ref, k_ref, v_ref, kk_ref, b_ref, h0_ref, y_ref, h_ref, *, tc):
    @pl.when(pl.program_id(1) == 0)
    def _():
        h_ref[...] = h0_ref[...]

    def key_row(ref, t, j):
        return ref[0, t, pl.ds(j, 1), :]

    def step(t, carry):
        parts = [None] * 4
        for j in range(RW_HD):
            term = h_ref[0, j] * key_row(kk_ref, t, j)
            parts[j % 4] = term if parts[j % 4] is None else parts[j % 4] + term
        sa = -((parts[0] + parts[1]) + (parts[2] + parts[3]))
        v = v_ref[0, t]
        parts = [None] * 4
        for j in range(RW_HD):
            h = h_ref[0, j] * key_row(w_ref, t, j) + sa * key_row(b_ref, t, j) + v * key_row(k_ref, t, j)
            h_ref[0, j] = h
            term = h * key_row(r_ref, t, j)
            parts[j % 4] = term if parts[j % 4] is None else parts[j % 4] + term
        y_ref[0, t] = (parts[0] + parts[1]) + (parts[2] + parts[3])
        return carry

    lax.fori_loop(0, tc, step, 0)


def _rw_scan(r, w, k, v, kk, b, h0, tc):
    nb, t, rows = v.shape[0], v.shape[1], v.shape[2]
    key = pl.BlockSpec((1, tc, RW_HD, LANES), lambda i, c: (i, c, 0, 0))
    val = pl.BlockSpec((1, tc, rows, LANES), lambda i, c: (i, c, 0, 0))
    st = pl.BlockSpec((1, RW_HD, rows, LANES), lambda i, c: (i, 0, 0, 0))
    return pl.pallas_call(
        functools.partial(_rw_scan_kernel, tc=tc),
        grid=(nb, t // tc),
        in_specs=[key, key, key, val, key, key, st],
        out_specs=[val, st],
        out_shape=[jax.ShapeDtypeStruct((nb, t, rows, LANES), F32),
                   jax.ShapeDtypeStruct((nb, RW_HD, rows, LANES), F32)],
        compiler_params=_cparams(("parallel", "arbitrary")),
        name="rw_scan",
    )(r, w, k, v, kk, b, h0)


def _scan_split(n_states):
    slices = max(1, LANES // n_states)
    assert (n_states * slices) % LANES == 0 and RW_HD % slices == 0 and (RW_HD // slices) % SUBLANES == 0, n_states
    return slices, LANES // slices


def _key_cols(x, slices, per_block):
    b, t, h, j = x.shape
    x = x.transpose(1, 3, 0, 2).reshape(t, j, (b * h) // per_block, 1, per_block)
    x = jnp.broadcast_to(x, (t, j, (b * h) // per_block, slices, per_block))
    return x.transpose(2, 0, 1, 3, 4).reshape((b * h) // per_block, t, j, LANES)


def _value_cols(x, slices, per_block):
    b, t, h, i = x.shape
    rows = i // slices
    x = x.reshape(b, t, h, slices, rows).transpose(1, 4, 3, 0, 2).reshape(t, rows, slices, (b * h) // per_block, per_block)
    return x.transpose(3, 0, 1, 2, 4).reshape((b * h) // per_block, t, rows, LANES)


def _value_cols_inv(y, b, h, slices, per_block):
    nb, t, rows, _ = y.shape
    y = y.reshape(nb, t, rows, slices, per_block).transpose(1, 2, 3, 0, 4).reshape(t, rows, slices, b, h)
    return y.transpose(3, 0, 4, 2, 1).reshape(b, t, h, slices * rows)


def _state_cols(s, slices, per_block):
    b, h, i, j = s.shape
    rows = i // slices
    s = s.reshape(b, h, slices, rows, j).transpose(4, 3, 2, 0, 1).reshape(j, rows, slices, (b * h) // per_block, per_block)
    return s.transpose(3, 0, 1, 2, 4).reshape((b * h) // per_block, j, rows, LANES)


def _state_cols_inv(hs, b, h, slices, per_block):
    nb, j, rows, _ = hs.shape
    hs = hs.reshape(nb, j, rows, slices, per_block).transpose(1, 2, 3, 0, 4).reshape(j, rows, slices, b, h)
    return hs.transpose(3, 4, 2, 1, 0).reshape(b, h, slices * rows, j)


def _out_proj_kernel(x_ref, oda_ref, y_ref, bonus_ref, g_ref, lnw_ref, lnb_ref, seg_ref, wo_ref, fn_ref, wq_ref,
                     h_ref, hn_ref, qp_ref):
    seg = seg_ref[...]
    y = y_ref[...]
    d = y - _head_sum(y, seg) * (1.0 / RW_HD)
    var = _head_sum(d * d, seg) * (1.0 / RW_HD)
    yn = d * lax.rsqrt(var + RW_GN_EPS) * lnw_ref[...] + lnb_ref[...]
    o_rw = (yn + bonus_ref[...]) * g_ref[...]
    h = (x_ref[...]
         + jnp.dot(oda_ref[...].astype(BF16), wo_ref[:DA_WIDTH, :], preferred_element_type=F32)
         + jnp.dot(o_rw.astype(BF16), wo_ref[DA_WIDTH:, :], preferred_element_type=F32))
    hn = _rms(h, fn_ref[...])
    h_ref[...] = h
    hn_ref[...] = hn
    qp_ref[...] = jnp.dot(hn.astype(BF16), wq_ref[...], preferred_element_type=F32)


def _out_proj(x, o_da, y, bonus, g, ln_w, ln_b, seg, w_out_bf, ffn_norm, wq_bf, tm):
    n = x.shape[0]
    row = lambda w: pl.BlockSpec((tm, w), lambda i: (i, 0))
    vec = lambda w: _full((1, w))
    return pl.pallas_call(
        _out_proj_kernel,
        grid=(n // tm,),
        in_specs=[row(D_MODEL), row(DA_WIDTH), row(RW_WIDTH), row(RW_WIDTH), row(RW_WIDTH),
                  vec(RW_WIDTH), vec(RW_WIDTH), _full((RW_WIDTH, RW_WIDTH)),
                  _full((DA_WIDTH + RW_WIDTH, D_MODEL)), vec(D_MODEL), _full((D_MODEL, PEER_QW))],
        out_specs=[row(D_MODEL), row(D_MODEL), row(PEER_QW)],
        out_shape=[jax.ShapeDtypeStruct((n, D_MODEL), F32), jax.ShapeDtypeStruct((n, D_MODEL), F32),
                   jax.ShapeDtypeStruct((n, PEER_QW), F32)],
        compiler_params=_cparams(("parallel",)),
        name="out_proj",
    )(x, o_da, y, bonus, g, ln_w, ln_b, seg, w_out_bf, ffn_norm, wq_bf)


def _top_k_rows(s, order, payload, k):
    vals, pays = [], []
    for _ in range(k):
        m = jnp.max(s, axis=0, keepdims=True)
        first = jnp.min(jnp.where(s == m, order, 1e9), axis=0, keepdims=True)
        hit = order == first
        vals.append(m)
        pays.append(first if payload is None else jnp.max(jnp.where(hit, payload, -1.0), axis=0, keepdims=True))
        s = jnp.where(hit, -jnp.inf, s)
    return jnp.concatenate(vals, axis=0), jnp.concatenate(pays, axis=0)


_PAIR_BLOCKS = ([("a", a0, 0) for a0 in (0, 8)] + [("a", 0, b) for b in range(1, 8)] + [("b", 0, 8)])


def _pair_candidates(sv, si, tm):
    r8 = lax.broadcasted_iota(jnp.int32, (SUBLANES, tm), 0)
    cand, flat, cidx = [], [], []
    for kind, a0, b0 in _PAIR_BLOCKS:
        if kind == "a":
            a, b = r8 + a0, jnp.full((SUBLANES, tm), b0, jnp.int32)
            val = sv[0][a0:a0 + SUBLANES] + sv[1][b0:b0 + 1]
            idx = si[0][a0:a0 + SUBLANES] * PEER_NKEYS + si[1][b0:b0 + 1]
        else:
            a, b = jnp.full((SUBLANES, tm), a0, jnp.int32), r8 + b0
            val = sv[0][a0:a0 + 1] + sv[1][b0:b0 + SUBLANES]
            idx = si[0][a0:a0 + 1] * PEER_NKEYS + si[1][b0:b0 + SUBLANES]
        ok = (a + 1) * (b + 1) <= PEER_TOPK
        cand.append(jnp.where(ok, val, -jnp.inf))
        flat.append(jnp.where(ok, a * PEER_TOPK + b, 1000000 + a * PEER_TOPK + b).astype(F32))
        cidx.append(idx)
    return jnp.concatenate(cand, axis=0), jnp.concatenate(flat, axis=0), jnp.concatenate(cidx, axis=0)


def _peer_topk_kernel(qp_ref, keys_ref, idx_ref, gate_ref, *, tm):
    key_row = lax.broadcasted_iota(jnp.int32, (PEER_NKEYS, tm), 0).astype(F32)
    dn = (((1,), (1,)), ((), ()))
    idx_rows, gate_rows = [], []
    for h in range(PEER_HEADS):
        sv, si = [], []
        for c in range(2):
            hc = 2 * h + c
            q = qp_ref[:, hc * PEER_KD:(hc + 1) * PEER_KD].astype(BF16)
            s = lax.dot_general(keys_ref[hc], q, dn, preferred_element_type=F32)
            v_, i_ = _top_k_rows(s, key_row, None, PEER_TOPK)
            sv.append(v_)
            si.append(i_)
        cand, flat, cidx = _pair_candidates(sv, si, tm)
        bv, eidx = _top_k_rows(cand, flat, cidx, PEER_TOPK)
        e = jnp.exp(bv - bv[0:1, :])
        gate_rows.append(e / jnp.sum(e, axis=0, keepdims=True))
        idx_rows.append(eidx)
    idx_ref[...] = jnp.concatenate(idx_rows, axis=0).T.astype(jnp.int32)
    gate_ref[...] = jnp.concatenate(gate_rows, axis=0).T


def _peer_topk(qp, keys_bf, tm):
    n = qp.shape[0]
    return pl.pallas_call(
        functools.partial(_peer_topk_kernel, tm=tm),
        grid=(n // tm,),
        in_specs=[pl.BlockSpec((tm, PEER_QW), lambda i: (i, 0)),
                  _full((PEER_HEADS * 2, PEER_NKEYS, PEER_KD))],
        out_specs=[pl.BlockSpec((tm, PEER_PICKS), lambda i: (i, 0))] * 2,
        out_shape=[jax.ShapeDtypeStruct((n, PEER_PICKS), jnp.int32),
                   jax.ShapeDtypeStruct((n, PEER_PICKS), F32)],
        compiler_params=_cparams(("parallel",)),
        name="peer_topk",
    )(qp, keys_bf)


GATE_PITCH = PEER_NKEYS + SUBLANES


def _peer_expert_kernel(idx_ref, gate_ref, hn_ref, h_ref, fn_ref, u_ref, v_ref, out_ref, g_ref, acc_ref, *, tm, te):
    c = pl.program_id(1)
    tiles = te // PEER_NKEYS
    dn = (((1,), (1,)), ((), ()))

    @pl.when(c == 0)
    def _():
        sub = lax.broadcasted_iota(jnp.int32, (PEER_NKEYS, PEER_PICKS), 0)

        def token(t, carry):
            e = idx_ref[pl.ds(t, 1), :]
            g = gate_ref[pl.ds(t, 1), :]
            g_hi = g.astype(BF16).astype(F32)
            hit1 = sub == lax.shift_right_logical(e, 7)
            hit2 = sub == (e & (PEER_NKEYS - 1))
            m1 = jnp.where(hit1, 1.0, 0.0).astype(BF16)
            m2_hi = jnp.where(hit2, g_hi, 0.0).astype(BF16)
            m2_lo = jnp.where(hit2, g - g_hi, 0.0).astype(BF16)
            dense = lax.dot_general(jnp.concatenate([m1, m1], axis=1), jnp.concatenate([m2_hi, m2_lo], axis=1), dn,
                                    preferred_element_type=F32)
            g_ref[pl.ds(pl.multiple_of(t * GATE_PITCH, SUBLANES), PEER_NKEYS), :] = dense
            return carry

        lax.fori_loop(0, tm, token, 0, unroll=4)

    lin = lax.dot_general(hn_ref[...].astype(BF16), u_ref[...], dn, preferred_element_type=F32)
    act = jax.nn.gelu(lin)
    parts = []
    for j in range(tiles):
        gates = g_ref[pl.ds(c * tiles + j, tm, stride=GATE_PITCH), :]
        parts.append((act[:, j * PEER_NKEYS:(j + 1) * PEER_NKEYS] * gates).astype(BF16))
    ffn = jnp.dot(jnp.concatenate(parts, axis=1), v_ref[...], preferred_element_type=F32)

    @pl.when(c == 0)
    def _():
        acc_ref[...] = ffn

    @pl.when(c != 0)
    def _():
        acc_ref[...] += ffn

    @pl.when(c == pl.num_programs(1) - 1)
    def _():
        out_ref[...] = _rms(h_ref[...] + acc_ref[...], fn_ref[...])


def _peer_expert(idx, gate, hn, h, final_norm, u_bf, v_bf, tm, te):
    n = hn.shape[0]
    n_exp = u_bf.shape[0]
    row = lambda w: pl.BlockSpec((tm, w), lambda i, c: (i, 0))
    tab = pl.BlockSpec((te, D_MODEL), lambda i, c: (c, 0))
    return pl.pallas_call(
        functools.partial(_peer_expert_kernel, tm=tm, te=te),
        grid=(n // tm, n_exp // te),
        in_specs=[row(PEER_PICKS), row(PEER_PICKS), row(D_MODEL), row(D_MODEL),
                  pl.BlockSpec((1, D_MODEL), lambda i, c: (0, 0)), tab, tab],
        out_specs=row(D_MODEL),
        out_shape=jax.ShapeDtypeStruct((n, D_MODEL), F32),
        scratch_shapes=[pltpu.VMEM((tm * GATE_PITCH, PEER_NKEYS), F32), pltpu.VMEM((tm, D_MODEL), F32)],
        compiler_params=_cparams(("parallel", "arbitrary")),
        name="peer_expert",
    )(idx, gate, hn, h, final_norm, u_bf, v_bf)


def _row_tile(n, target):
    tm = min(n, target)
    while n % tm:
        tm //= 2
    return tm


def _group(x, pos, attend, wkv0, shift0, wts):
    b, t, _ = x.shape
    n = b * t
    x2 = x.reshape(n, D_MODEL)
    tm = _row_tile(n, 256)
    cos, sin = _rope_tables(jnp.tile(pos, b))
    q, k, v, p_rw = _in_proj(x2, wts["attn_norm"], wts["w_in"], cos, sin, tm)

    o_da = attend(q.reshape(b, t, DA_QK), k.reshape(b, t, DA_QK), v.reshape(b, t, DA_WIDTH))

    rows = _row_tile(t, 256)
    p3 = p_rw.reshape(n // rows, rows, RW_PROJ)
    p_bt = p_rw.reshape(b, t, RW_PROJ)
    prev_rows = p_bt[:, rows - 1::rows][:, :t // rows - 1]
    first = jnp.concatenate([shift0[:, None], prev_rows], axis=1).reshape(n // rows, 1, RW_PROJ)
    gb = _row_tile(n // rows, max(1, 256 // rows))
    r, w, k_mod, v_rw, kk, bb_, g, bonus = _rw_pre(
        p3, first, wts["rw_mu"], wts["rw_wcat"], wts["rw_w0"], wts["rw_a0"],
        wts["rw_k_k"], wts["rw_k_a"], wts["rw_r_k"], wts["seg"], gb)
    slices, per_block = _scan_split(b * RW_HEADS)
    heads = lambda a: a.reshape(b, t, RW_HEADS, RW_HD)
    key = lambda a: _key_cols(heads(a), slices, per_block)
    y_cols, h_cols = _rw_scan(key(r), key(w), key(k_mod), _value_cols(heads(v_rw), slices, per_block),
                              key(kk), key(bb_), _state_cols(wkv0, slices, per_block), tc=_row_tile(t, 64))
    y = _value_cols_inv(y_cols, b, RW_HEADS, slices, per_block)
    wkv = _state_cols_inv(h_cols, b, RW_HEADS, slices, per_block)

    h, hn, qp = _out_proj(x2, o_da.reshape(n, DA_WIDTH), y.reshape(n, RW_WIDTH), bonus, g,
                          wts["rw_ln_w"], wts["rw_ln_b"], wts["seg"], wts["w_out"], wts["ffn_norm"],
                          wts["peer_wq"], tm)
    idx, gate = _peer_topk(qp, wts["peer_keys"], tm)
    out = _peer_expert(idx, gate, hn, h, wts["final_norm"], wts["peer_u"], wts["peer_v"], tm, te=1024)
    return (out.reshape(b, t, D_MODEL), k.reshape(b, t, DA_HEADS, 2 * DA_HD), v.reshape(b, t, DA_HEADS, DA_VD),
            wkv, p_bt[:, -1])


def kernel(x_prompt, x_sample, cache_k, cache_v, state_wkv, state_shift, page_table, attn_norm, w_in, w_out,
           da_lambda_q1, da_lambda_k1, da_lambda_q2, da_lambda_k2, da_subln, rw_mu, rw_w0, rw_w_up, rw_a0,
           rw_a_up, rw_g_up, rw_k_k, rw_k_a, rw_r_k, rw_ln_w, rw_ln_b, ffn_norm, peer_wq, peer_sub_keys,
           peer_u, peer_v, final_norm):
    assert w_in.shape[0] == 1, "single-layer trunk"
    b, t_p, _ = x_prompt.shape
    bd, t_s, _ = x_sample.shape
    past = page_table.shape[1] * PAGE_SIZE
    vec = lambda a: a.reshape(1, -1).astype(F32)

    wcat = jnp.zeros((RW_LORA, 3 * RW_WIDTH), F32)
    wcat = wcat.at[:RW_W_RANK, :RW_WIDTH].set(rw_w_up[0])
    wcat = wcat.at[RW_W_RANK:RW_W_RANK + RW_A_RANK, RW_WIDTH:2 * RW_WIDTH].set(rw_a_up[0])
    wcat = wcat.at[RW_W_RANK + RW_A_RANK:, 2 * RW_WIDTH:].set(rw_g_up[0])
    lane_head = jnp.arange(RW_WIDTH) // RW_HD
    wts = dict(
        attn_norm=vec(attn_norm[0]), w_in=w_in[0].astype(BF16), w_out=w_out[0].astype(BF16),
        rw_mu=vec(rw_mu[0]), rw_wcat=wcat.astype(BF16), rw_w0=vec(rw_w0[0]), rw_a0=vec(rw_a0[0]),
        rw_k_k=vec(rw_k_k[0]), rw_k_a=vec(rw_k_a[0]), rw_r_k=vec(rw_r_k[0]),
        rw_ln_w=vec(rw_ln_w[0]), rw_ln_b=vec(rw_ln_b[0]),
        seg=(lane_head[:, None] == lane_head[None, :]).astype(F32),
        ffn_norm=vec(ffn_norm[0]), peer_wq=peer_wq[0].astype(BF16),
        peer_keys=peer_sub_keys[0].reshape(PEER_HEADS * 2, PEER_NKEYS, PEER_KD).astype(BF16),
        peer_u=peer_u.reshape(-1, D_MODEL).astype(BF16), peer_v=peer_v.reshape(-1, D_MODEL).astype(BF16),
        final_norm=vec(final_norm),
    )
    lams = (vec(da_lambda_q1[0]), vec(da_lambda_k1[0]), vec(da_lambda_q2[0]), vec(da_lambda_k2[0]))
    subln = vec(da_subln[0])

    att_p = lambda q, k, v: _attn_prompt(q, k, v, lams, subln, _row_tile(t_p, 512))
    ck = cache_k.reshape(-1, DA_VD)
    cv = cache_v.reshape(-1, DA_VD)
    att_s = lambda q, k, v: _attn_sample(q, k, v, ck, cv, page_table, lams, subln)

    pos_p = jnp.arange(t_p, dtype=jnp.int32)
    pos_s = past + jnp.arange(t_s, dtype=jnp.int32)
    yp, kp, vp, wp, sp = _group(x_prompt, pos_p, att_p, jnp.zeros((b, RW_HEADS, RW_HD, RW_HD), F32),
                                jnp.zeros((b, RW_PROJ), F32), wts)
    ys, ks, vs, ws, ss = _group(x_sample, pos_s, att_s, state_wkv[0], state_shift[0], wts)
    return (yp, ys, kp[None], vp[None], wp[None], sp[None], ks[None], vs[None], ws[None], ss[None])
```

```python
import functools
import math

import jax
import jax.numpy as jnp
from jax import lax
from jax.experimental import pallas as pl
from jax.experimental.pallas import tpu as pltpu

F32 = jnp.float32
BF16 = jnp.bfloat16

D_MODEL = 1024
PAGE_SIZE = 128
DA_HEADS = 4
DA_HD = 64
DA_VD = 2 * DA_HD
DA_WIDTH = DA_HEADS * DA_VD
DA_QK = DA_HEADS * 2 * DA_HD
RW_HEADS = 8
RW_HD = 64
RW_WIDTH = RW_HEADS * RW_HD
RW_W_RANK = 64
RW_A_RANK = 64
RW_G_RANK = 128
RW_LORA = RW_W_RANK + RW_A_RANK + RW_G_RANK
RW_PROJ = 3 * RW_WIDTH + RW_LORA
RW_GN_EPS = 64e-5
IN_PROJ = 2 * DA_QK + DA_WIDTH + RW_PROJ
PEER_HEADS = 8
PEER_NKEYS = 128
PEER_KD = 128
PEER_TOPK = 16
PEER_PICKS = PEER_HEADS * PEER_TOPK
PEER_QW = PEER_HEADS * 2 * PEER_KD
ROPE_THETA = 10000.0
NORM_EPS = 1e-6
NEG_INF = -1e30
LAM_INIT = 0.8 - 0.6 * math.exp(-0.3 * 0)

LANES = 128
SUBLANES = 8
VMEM_LIMIT_BYTES = 56 * 1024 * 1024

HIGHEST = lax.Precision.HIGHEST


def _cparams(sem):
    return pltpu.CompilerParams(dimension_semantics=sem, vmem_limit_bytes=VMEM_LIMIT_BYTES)


def _full(shape):
    return pl.BlockSpec(shape, lambda *_: (0,) * len(shape))


def _rms(x, g):
    return x * lax.rsqrt(jnp.mean(x * x, axis=-1, keepdims=True) + NORM_EPS) * g


def _sigmoid(x):
    return 1.0 / (1.0 + jnp.exp(-x))


def _rope_slab(t, cos, sin_signed):
    lane = lax.broadcasted_iota(jnp.int32, t.shape, 1)
    swapped = jnp.where(lane % DA_HD < DA_HD // 2,
                        pltpu.roll(t, LANES - DA_HD // 2, 1),
                        pltpu.roll(t, DA_HD // 2, 1))
    return t * cos + swapped * sin_signed


def _in_proj_kernel(x_ref, g_ref, w_ref, cos_ref, sin_ref, q_ref, k_ref, v_ref, p_ref):
    xn = _rms(x_ref[...], g_ref[...])
    proj = jnp.dot(xn.astype(BF16), w_ref[...], preferred_element_type=F32)
    cos = cos_ref[...]
    sin = sin_ref[...]
    for s in range(DA_QK // LANES):
        lo = s * LANES
        q_ref[:, lo:lo + LANES] = _rope_slab(proj[:, lo:lo + LANES], cos, sin)
        k_ref[:, lo:lo + LANES] = _rope_slab(proj[:, DA_QK + lo:DA_QK + lo + LANES], cos, sin)
    v_ref[...] = proj[:, 2 * DA_QK:2 * DA_QK + DA_WIDTH]
    p_ref[...] = proj[:, 2 * DA_QK + DA_WIDTH:]


def _in_proj(x, g, w_bf, cos, sin, tm):
    n = x.shape[0]
    row = lambda w: pl.BlockSpec((tm, w), lambda i: (i, 0))
    return pl.pallas_call(
        _in_proj_kernel,
        grid=(n // tm,),
        in_specs=[row(D_MODEL), _full((1, D_MODEL)), _full((D_MODEL, IN_PROJ)), row(LANES), row(LANES)],
        out_specs=[row(DA_QK), row(DA_QK), row(DA_WIDTH), row(RW_PROJ)],
        out_shape=[jax.ShapeDtypeStruct((n, DA_QK), F32), jax.ShapeDtypeStruct((n, DA_QK), F32),
                   jax.ShapeDtypeStruct((n, DA_WIDTH), F32), jax.ShapeDtypeStruct((n, RW_PROJ), F32)],
        compiler_params=_cparams(("parallel",)),
        name="in_proj",
    )(x, g, w_bf, cos, sin)


def _rope_tables(pos):
    half = DA_HD // 2
    inv = ROPE_THETA ** (-jnp.arange(half, dtype=F32) / half)
    ang = pos.astype(F32)[:, None] * inv[None, :]
    cos, sin = jnp.cos(ang), jnp.sin(ang)
    cos = jnp.tile(cos, (1, LANES // half))
    sin = jnp.tile(jnp.concatenate([-sin, sin], axis=1), (1, LANES // DA_HD))
    return cos, sin


def _lambda(lq1, lk1, lq2, lk2):
    return (jnp.exp(jnp.sum(lq1 * lk1, axis=-1, keepdims=True))
            - jnp.exp(jnp.sum(lq2 * lk2, axis=-1, keepdims=True)) + LAM_INIT)


def _attn_prompt_kernel(qi_ref, ki_ref, q_ref, k_ref, v_ref, lq1, lk1, lq2, lk2, g_ref, o_ref,
                        m1, l1, a1, m2, l2, a2, *, tq):
    qi = qi_ref[pl.program_id(2)]
    ki = ki_ref[pl.program_id(2)]
    maps = ((m1, l1, a1), (m2, l2, a2))

    @pl.when(ki == 0)
    def _():
        for m, l, a in maps:
            m[...] = jnp.full(m.shape, NEG_INF, F32)
            l[...] = jnp.zeros(l.shape, F32)
            a[...] = jnp.zeros(a.shape, F32)

    def block(diagonal):
        q = q_ref[0] * (DA_HD ** -0.5)
        lane = lax.broadcasted_iota(jnp.int32, q.shape, 1)
        q1 = jnp.where(lane < DA_HD, q, 0.0).astype(BF16)
        q2 = jnp.where(lane < DA_HD, 0.0, q).astype(BF16)
        kb = k_ref[0].astype(BF16)
        vb = v_ref[0].astype(BF16)
        dn = (((1,), (1,)), ((), ()))
        state = [(m[...], l[...], a[...]) for m, l, a in maps]
        new = []
        for qm, (m_old, l_old, a_old) in zip((q1, q2), state):
            s = lax.dot_general(qm, kb, dn, preferred_element_type=F32)
            if diagonal:
                row = lax.broadcasted_iota(jnp.int32, (tq, tq), 0)
                col = lax.broadcasted_iota(jnp.int32, (tq, tq), 1)
                s = jnp.where(col <= row, s, NEG_INF)
            m_new = jnp.maximum(m_old, jnp.max(s, axis=-1, keepdims=True))
            alpha = jnp.exp(m_old - m_new)
            p = jnp.exp(s - m_new)
            new.append((m_new, alpha * l_old + jnp.sum(p, axis=-1, keepdims=True),
                        alpha * a_old + jnp.dot(p.astype(BF16), vb, preferred_element_type=F32)))
        for (m, l, a), (m_new, l_new, a_new) in zip(maps, new):
            m[...] = m_new
            l[...] = l_new
            a[...] = a_new

    @pl.when(ki < qi)
    def _():
        block(diagonal=False)

    @pl.when(ki == qi)
    def _():
        block(diagonal=True)
        lam = _lambda(lq1[...], lk1[...], lq2[...], lk2[...])
        o = a1[...] / l1[...] - lam * (a2[...] / l2[...])
        o_ref[0] = _rms(o, g_ref[...]) * (1.0 - LAM_INIT)


def _attn_prompt(q, k, v, lams, subln, tq):
    b, t, _ = q.shape
    nq = t // tq
    pairs = [(i, j) for i in range(nq) for j in range(i + 1)]
    qi_tab = jnp.asarray([i for i, _ in pairs], jnp.int32)
    ki_tab = jnp.asarray([j for _, j in pairs], jnp.int32)
    qspec = pl.BlockSpec((1, tq, DA_VD), lambda bi, h, p, qi, ki: (bi, qi[p], h))
    kspec = pl.BlockSpec((1, tq, DA_VD), lambda bi, h, p, qi, ki: (bi, ki[p], h))
    vec = lambda w: pl.BlockSpec((1, w), lambda bi, h, p, qi, ki: (0, 0))
    grid_spec = pltpu.PrefetchScalarGridSpec(
        num_scalar_prefetch=2,
        grid=(b, DA_HEADS, len(pairs)),
        in_specs=[qspec, kspec, kspec, vec(DA_HD), vec(DA_HD), vec(DA_HD), vec(DA_HD), vec(DA_VD)],
        out_specs=qspec,
        scratch_shapes=[pltpu.VMEM((tq, 1), F32), pltpu.VMEM((tq, 1), F32), pltpu.VMEM((tq, DA_VD), F32),
                        pltpu.VMEM((tq, 1), F32), pltpu.VMEM((tq, 1), F32), pltpu.VMEM((tq, DA_VD), F32)],
    )
    return pl.pallas_call(
        functools.partial(_attn_prompt_kernel, tq=tq),
        grid_spec=grid_spec,
        out_shape=jax.ShapeDtypeStruct((b, t, DA_WIDTH), F32),
        compiler_params=_cparams(("parallel", "parallel", "arbitrary")),
        name="attn_prompt",
    )(qi_tab, ki_tab, q, k, v, *lams, subln)


def _attn_sample_kernel(pt_ref, q_ref, *refs, n_steps, pps, t_new):
    ck_refs, cv_refs = refs[:pps], refs[pps:2 * pps]
    kn_ref, vn_ref, lq1, lk1, lq2, lk2, g_ref, o_ref, qb_ref, m_ref, l_ref, acc_ref = refs[2 * pps:]
    p = pl.program_id(1)
    dn = (((1,), (1,)), ((), ()))

    @pl.when(p == 0)
    def _():
        q = q_ref[0] * (DA_HD ** -0.5)
        lane = lax.broadcasted_iota(jnp.int32, (t_new, DA_VD), 1)
        for h in range(DA_HEADS):
            slab = q[:, h * DA_VD:(h + 1) * DA_VD]
            qb_ref[h, 0:t_new, :] = jnp.where(lane < DA_HD, slab, 0.0)
            qb_ref[h, t_new:2 * t_new, :] = jnp.where(lane < DA_HD, 0.0, slab)
        m_ref[...] = jnp.full(m_ref.shape, NEG_INF, F32)
        l_ref[...] = jnp.zeros(l_ref.shape, F32)
        acc_ref[...] = jnp.zeros(acc_ref.shape, F32)

    @pl.when(p < n_steps)
    def _():
        head_rows = [pl.ds(h, PAGE_SIZE, stride=DA_HEADS) for h in range(DA_HEADS)]
        m_old = [m_ref[h] for h in range(DA_HEADS)]
        l_old = [l_ref[h] for h in range(DA_HEADS)]
        acc_old = [acc_ref[h] for h in range(DA_HEADS)]
        qb = [qb_ref[h].astype(BF16) for h in range(DA_HEADS)]
        s = [jnp.concatenate([lax.dot_general(qb[h], ck[head_rows[h], :].astype(BF16), dn, preferred_element_type=F32)
                              for ck in ck_refs], axis=1) for h in range(DA_HEADS)]
        m_new = [jnp.maximum(m_old[h], jnp.max(s[h], axis=-1, keepdims=True)) for h in range(DA_HEADS)]
        alpha = [jnp.exp(m_old[h] - m_new[h]) for h in range(DA_HEADS)]
        pr = [jnp.exp(s[h] - m_new[h]) for h in range(DA_HEADS)]
        l_new = [alpha[h] * l_old[h] + jnp.sum(pr[h], axis=-1, keepdims=True) for h in range(DA_HEADS)]
        acc_new = []
        for h in range(DA_HEADS):
            pv = None
            for i, cv in enumerate(cv_refs):
                term = jnp.dot(pr[h][:, i * PAGE_SIZE:(i + 1) * PAGE_SIZE].astype(BF16),
                               cv[head_rows[h], :].astype(BF16), preferred_element_type=F32)
                pv = term if pv is None else pv + term
            acc_new.append(alpha[h] * acc_old[h] + pv)
        for h in range(DA_HEADS):
            m_ref[h] = m_new[h]
            l_ref[h] = l_new[h]
            acc_ref[h] = acc_new[h]

    @pl.when(p == n_steps)
    def _():
        lam = _lambda(lq1[...], lk1[...], lq2[...], lk2[...])
        t_of_row = lax.broadcasted_iota(jnp.int32, (2 * t_new, 1), 0) % t_new
        for h in range(DA_HEADS):
            qb = qb_ref[h]
            kn = kn_ref[0, :, h * DA_VD:(h + 1) * DA_VD]
            vn = vn_ref[0, :, h * DA_VD:(h + 1) * DA_VD]
            s_new = [jnp.where(t_of_row >= j, jnp.sum(qb * kn[j:j + 1, :], axis=-1, keepdims=True), NEG_INF)
                     for j in range(t_new)]
            m_old = m_ref[h]
            m_new = m_old
            for s_j in s_new:
                m_new = jnp.maximum(m_new, s_j)
            alpha = jnp.exp(m_old - m_new)
            l = alpha * l_ref[h]
            acc = alpha * acc_ref[h]
            for j, s_j in enumerate(s_new):
                p_j = jnp.exp(s_j - m_new)
                l = l + p_j
                acc = acc + p_j * vn[j:j + 1, :]
            o_all = acc / l
            o = o_all[0:t_new] - lam * o_all[t_new:2 * t_new]
            o_ref[0, :, h * DA_VD:(h + 1) * DA_VD] = _rms(o, g_ref[...]) * (1.0 - LAM_INIT)


def _attn_sample(q, k_new, v_new, cache_k, cache_v, page_table, lams, subln):
    bd, t_new, _ = q.shape
    n_pages = page_table.shape[1]
    pps = next(c for c in (8, 4, 2, 1) if n_pages % c == 0)
    n_steps = n_pages // pps
    page_rows = PAGE_SIZE * DA_HEADS
    tok = pl.BlockSpec((1, t_new, DA_WIDTH), lambda b, p, pt: (b, 0, 0))

    def page(i):
        return pl.BlockSpec((page_rows, DA_VD),
                            lambda b, p, pt: (pt[b, jnp.minimum(p, n_steps - 1) * pps + i], 0))

    pages = [page(i) for i in range(pps)]
    vec = lambda w: pl.BlockSpec((1, w), lambda b, p, pt: (0, 0))
    rows = 2 * t_new
    grid_spec = pltpu.PrefetchScalarGridSpec(
        num_scalar_prefetch=1,
        grid=(bd, n_steps + 1),
        in_specs=[tok] + pages + pages + [tok, tok, vec(DA_HD), vec(DA_HD), vec(DA_HD), vec(DA_HD), vec(DA_VD)],
        out_specs=tok,
        scratch_shapes=[pltpu.VMEM((DA_HEADS, rows, DA_VD), F32), pltpu.VMEM((DA_HEADS, rows, 1), F32),
                        pltpu.VMEM((DA_HEADS, rows, 1), F32), pltpu.VMEM((DA_HEADS, rows, DA_VD), F32)],
    )
    return pl.pallas_call(
        functools.partial(_attn_sample_kernel, n_steps=n_steps, pps=pps, t_new=t_new),
        grid_spec=grid_spec,
        out_shape=jax.ShapeDtypeStruct((bd, t_new, DA_WIDTH), F32),
        compiler_params=_cparams(("parallel", "arbitrary")),
        name="attn_sample",
    )(page_table, q, *([cache_k] * pps), *([cache_v] * pps), k_new, v_new, *lams, subln)


def _head_sum(x, seg):
    return jnp.dot(x, seg, preferred_element_type=F32, precision=HIGHEST)


def _rw_pre_kernel(p_ref, first_ref, mu_ref, wcat_ref, w0_ref, a0_ref, kk_ref, ka_ref, rk_ref, seg_ref,
                   r_out, w_out, k_out, v_out, kk_out, b_out, g_out, bonus_out):
    p = p_ref[...]
    gb, rows, _ = p.shape
    row = lax.broadcasted_iota(jnp.int32, p.shape, 1)
    p_prev = jnp.where(row == 0, first_ref[...], pltpu.roll(p, 1, 1))
    ps = (p + (p_prev - p) * mu_ref[...]).reshape(gb * rows, RW_PROJ)
    w_ = RW_WIDTH
    r, k, v = ps[:, :w_], ps[:, w_:2 * w_], ps[:, 2 * w_:3 * w_]
    x = ps[:, 3 * w_:]
    lane = lax.broadcasted_iota(jnp.int32, x.shape, 1)
    act = jnp.where(lane < RW_W_RANK, jnp.tanh(x), jnp.where(lane < RW_W_RANK + RW_A_RANK, x, _sigmoid(x)))
    lora = jnp.dot(act.astype(BF16), wcat_ref[...], preferred_element_type=F32)
    decay = jnp.exp(-math.exp(-0.5) * _sigmoid(w0_ref[...] + lora[:, :w_]))
    a = _sigmoid(a0_ref[...] + lora[:, w_:2 * w_])
    seg = seg_ref[...]
    kk = k * kk_ref[...]
    kk = kk / jnp.maximum(jnp.sqrt(_head_sum(kk * kk, seg)), 1e-12)
    k_mod = k * (1.0 + (a - 1.0) * ka_ref[...])
    r_out[...] = r
    w_out[...] = decay
    k_out[...] = k_mod
    v_out[...] = v
    kk_out[...] = kk
    b_out[...] = kk * a
    g_out[...] = lora[:, 2 * w_:]
    bonus_out[...] = _head_sum(r * k_mod * rk_ref[...], seg) * v


def _rw_pre(p3, first, mu, wcat, w0, a0, k_k, k_a, r_k, seg, gb):
    g, rows, _ = p3.shape
    n = g * rows
    vec = lambda w: _full((1, w))
    out = pl.BlockSpec((gb * rows, RW_WIDTH), lambda i: (i, 0))
    return pl.pallas_call(
        _rw_pre_kernel,
        grid=(g // gb,),
        in_specs=[pl.BlockSpec((gb, rows, RW_PROJ), lambda i: (i, 0, 0)),
                  pl.BlockSpec((gb, 1, RW_PROJ), lambda i: (i, 0, 0)),
                  vec(RW_PROJ), _full((RW_LORA, 3 * RW_WIDTH)), vec(RW_WIDTH), vec(RW_WIDTH),
                  vec(RW_WIDTH), vec(RW_WIDTH), vec(RW_WIDTH), _full((RW_WIDTH, RW_WIDTH))],
        out_specs=[out] * 8,
        out_shape=[jax.ShapeDtypeStruct((n, RW_WIDTH), F32)] * 8,
        compiler_params=_cparams(("parallel",)),
        name="rw_pre",
    )(p3, first, mu, wcat, w0, a0, k_k, k_a, r_k, seg)


def _rw_scan_kernel(r_ref, w_ref, k_ref, v_ref, kk_ref, b_ref, h0_ref, y_ref, h_ref, *, tc):
    @pl.when(pl.program_id(1) == 0)
    def _():
        h_ref[...] = h0_ref[...]

    def key_row(ref, t, j):
        return ref[0, t, pl.ds(j, 1), :]

    def step(t, carry):
        parts = [None] * 4
        for j in range(RW_HD):
            term = h_ref[0, j] * key_row(kk_ref, t, j)
            parts[j % 4] = term if parts[j % 4] is None else parts[j % 4] + term
        sa = -((parts[0] + parts[1]) + (parts[2] + parts[3]))
        v = v_ref[0, t]
        parts = [None] * 4
        for j in range(RW_HD):
            h = h_ref[0, j] * key_row(w_ref, t, j) + sa * key_row(b_ref, t, j) + v * key_row(k_ref, t, j)
            h_ref[0, j] = h
            term = h * key_row(r_ref, t, j)
            parts[j % 4] = term if parts[j % 4] is None else parts[j % 4] + term
        y_ref[0, t] = (parts[0] + parts[1]) + (parts[2] + parts[3])
        return carry

    lax.fori_loop(0, tc, step, 0)


def _rw_scan(r, w, k, v, kk, b, h0, tc):
    nb, t, rows = v.shape[0], v.shape[1], v.shape[2]
    key = pl.BlockSpec((1, tc, RW_HD, LANES), lambda i, c: (i, c, 0, 0))
    val = pl.BlockSpec((1, tc, rows, LANES), lambda i, c: (i, c, 0, 0))
    st = pl.BlockSpec((1, RW_HD, rows, LANES), lambda i, c: (i, 0, 0, 0))
    return pl.pallas_call(
        functools.partial(_rw_scan_kernel, tc=tc),
        grid=(nb, t // tc),
        in_specs=[key, key, key, val, key, key, st],
        out_specs=[val, st],
        out_shape=[jax.ShapeDtypeStruct((nb, t, rows, LANES), F32),
                   jax.ShapeDtypeStruct((nb, RW_HD, rows, LANES), F32)],
        compiler_params=_cparams(("parallel", "arbitrary")),
        name="rw_scan",
    )(r, w, k, v, kk, b, h0)


def _scan_split(n_states):
    slices = max(1, LANES // n_states)
    assert (n_states * slices) % LANES == 0 and RW_HD % slices == 0 and (RW_HD // slices) % SUBLANES == 0, n_states
    return slices, LANES // slices


def _key_cols(x, slices, per_block):
    b, t, h, j = x.shape
    x = x.transpose(1, 3, 0, 2).reshape(t, j, (b * h) // per_block, 1, per_block)
    x = jnp.broadcast_to(x, (t, j, (b * h) // per_block, slices, per_block))
    return x.transpose(2, 0, 1, 3, 4).reshape((b * h) // per_block, t, j, LANES)


def _value_cols(x, slices, per_block):
    b, t, h, i = x.shape
    rows = i // slices
    x = x.reshape(b, t, h, slices, rows).transpose(1, 4, 3, 0, 2).reshape(t, rows, slices, (b * h) // per_block, per_block)
    return x.transpose(3, 0, 1, 2, 4).reshape((b * h) // per_block, t, rows, LANES)


def _value_cols_inv(y, b, h, slices, per_block):
    nb, t, rows, _ = y.shape
    y = y.reshape(nb, t, rows, slices, per_block).transpose(1, 2, 3, 0, 4).reshape(t, rows, slices, b, h)
    return y.transpose(3, 0, 4, 2, 1).reshape(b, t, h, slices * rows)


def _state_cols(s, slices, per_block):
    b, h, i, j = s.shape
    rows = i // slices
    s = s.reshape(b, h, slices, rows, j).transpose(4, 3, 2, 0, 1).reshape(j, rows, slices, (b * h) // per_block, per_block)
    return s.transpose(3, 0, 1, 2, 4).reshape((b * h) // per_block, j, rows, LANES)


def _state_cols_inv(hs, b, h, slices, per_block):
    nb, j, rows, _ = hs.shape
    hs = hs.reshape(nb, j, rows, slices, per_block).transpose(1, 2, 3, 0, 4).reshape(j, rows, slices, b, h)
    return hs.transpose(3, 4, 2, 1, 0).reshape(b, h, slices * rows, j)


def _out_proj_kernel(x_ref, oda_ref, y_ref, bonus_ref, g_ref, lnw_ref, lnb_ref, seg_ref, wo_ref, fn_ref, wq_ref,
                     h_ref, hn_ref, qp_ref):
    seg = seg_ref[...]
    y = y_ref[...]
    d = y - _head_sum(y, seg) * (1.0 / RW_HD)
    var = _head_sum(d * d, seg) * (1.0 / RW_HD)
    yn = d * lax.rsqrt(var + RW_GN_EPS) * lnw_ref[...] + lnb_ref[...]
    o_rw = (yn + bonus_ref[...]) * g_ref[...]
    h = (x_ref[...]
         + jnp.dot(oda_ref[...].astype(BF16), wo_ref[:DA_WIDTH, :], preferred_element_type=F32)
         + jnp.dot(o_rw.astype(BF16), wo_ref[DA_WIDTH:, :], preferred_element_type=F32))
    hn = _rms(h, fn_ref[...])
    h_ref[...] = h
    hn_ref[...] = hn
    qp_ref[...] = jnp.dot(hn.astype(BF16), wq_ref[...], preferred_element_type=F32)


def _out_proj(x, o_da, y, bonus, g, ln_w, ln_b, seg, w_out_bf, ffn_norm, wq_bf, tm):
    n = x.shape[0]
    row = lambda w: pl.BlockSpec((tm, w), lambda i: (i, 0))
    vec = lambda w: _full((1, w))
    return pl.pallas_call(
        _out_proj_kernel,
        grid=(n // tm,),
        in_specs=[row(D_MODEL), row(DA_WIDTH), row(RW_WIDTH), row(RW_WIDTH), row(RW_WIDTH),
                  vec(RW_WIDTH), vec(RW_WIDTH), _full((RW_WIDTH, RW_WIDTH)),
                  _full((DA_WIDTH + RW_WIDTH, D_MODEL)), vec(D_MODEL), _full((D_MODEL, PEER_QW))],
        out_specs=[row(D_MODEL), row(D_MODEL), row(PEER_QW)],
        out_shape=[jax.ShapeDtypeStruct((n, D_MODEL), F32), jax.ShapeDtypeStruct((n, D_MODEL), F32),
                   jax.ShapeDtypeStruct((n, PEER_QW), F32)],
        compiler_params=_cparams(("parallel",)),
        name="out_proj",
    )(x, o_da, y, bonus, g, ln_w, ln_b, seg, w_out_bf, ffn_norm, wq_bf)


def _top_k_rows(s, order, payload, k):
    vals, pays = [], []
    for _ in range(k):
        m = jnp.max(s, axis=0, keepdims=True)
        first = jnp.min(jnp.where(s == m, order, 1e9), axis=0, keepdims=True)
        hit = order == first
        vals.append(m)
        pays.append(first if payload is None else jnp.max(jnp.where(hit, payload, -1.0), axis=0, keepdims=True))
        s = jnp.where(hit, -jnp.inf, s)
    return jnp.concatenate(vals, axis=0), jnp.concatenate(pays, axis=0)


_PAIR_BLOCKS = ([("a", a0, 0) for a0 in (0, 8)] + [("a", 0, b) for b in range(1, 8)] + [("b", 0, 8)])


def _pair_candidates(sv, si, tm):
    r8 = lax.broadcasted_iota(jnp.int32, (SUBLANES, tm), 0)
    cand, flat, cidx = [], [], []
    for kind, a0, b0 in _PAIR_BLOCKS:
        if kind == "a":
            a, b = r8 + a0, jnp.full((SUBLANES, tm), b0, jnp.int32)
            val = sv[0][a0:a0 + SUBLANES] + sv[1][b0:b0 + 1]
            idx = si[0][a0:a0 + SUBLANES] * PEER_NKEYS + si[1][b0:b0 + 1]
        else:
            a, b = jnp.full((SUBLANES, tm), a0, jnp.int32), r8 + b0
            val = sv[0][a0:a0 + 1] + sv[1][b0:b0 + SUBLANES]
            idx = si[0][a0:a0 + 1] * PEER_NKEYS + si[1][b0:b0 + SUBLANES]
        ok = (a + 1) * (b + 1) <= PEER_TOPK
        cand.append(jnp.where(ok, val, -jnp.inf))
        flat.append(jnp.where(ok, a * PEER_TOPK + b, 1000000 + a * PEER_TOPK + b).astype(F32))
        cidx.append(idx)
    return jnp.concatenate(cand, axis=0), jnp.concatenate(flat, axis=0), jnp.concatenate(cidx, axis=0)


def _peer_topk_kernel(qp_ref, keys_ref, idx_ref, gate_ref, *, tm):
    key_row = lax.broadcasted_iota(jnp.int32, (PEER_NKEYS, tm), 0).astype(F32)
    dn = (((1,), (1,)), ((), ()))
    idx_rows, gate_rows = [], []
    for h in range(PEER_HEADS):
        sv, si = [], []
        for c in range(2):
            hc = 2 * h + c
            q = qp_ref[:, hc * PEER_KD:(hc + 1) * PEER_KD].astype(BF16)
            s = lax.dot_general(keys_ref[hc], q, dn, preferred_element_type=F32)
            v_, i_ = _top_k_rows(s, key_row, None, PEER_TOPK)
            sv.append(v_)
            si.append(i_)
        cand, flat, cidx = _pair_candidates(sv, si, tm)
        bv, eidx = _top_k_rows(cand, flat, cidx, PEER_TOPK)
        e = jnp.exp(bv - bv[0:1, :])
        gate_rows.append(e / jnp.sum(e, axis=0, keepdims=True))
        idx_rows.append(eidx)
    idx_ref[...] = jnp.concatenate(idx_rows, axis=0).T.astype(jnp.int32)
    gate_ref[...] = jnp.concatenate(gate_rows, axis=0).T


def _peer_topk(qp, keys_bf, tm):
    n = qp.shape[0]
    return pl.pallas_call(
        functools.partial(_peer_topk_kernel, tm=tm),
        grid=(n // tm,),
        in_specs=[pl.BlockSpec((tm, PEER_QW), lambda i: (i, 0)),
                  _full((PEER_HEADS * 2, PEER_NKEYS, PEER_KD))],
        out_specs=[pl.BlockSpec((tm, PEER_PICKS), lambda i: (i, 0))] * 2,
        out_shape=[jax.ShapeDtypeStruct((n, PEER_PICKS), jnp.int32),
                   jax.ShapeDtypeStruct((n, PEER_PICKS), F32)],
        compiler_params=_cparams(("parallel",)),
        name="peer_topk",
    )(qp, keys_bf)


GATE_PITCH = PEER_NKEYS + SUBLANES


def _peer_expert_kernel(idx_ref, gate_ref, hn_ref, h_ref, fn_ref, u_ref, v_ref, out_ref, g_ref, acc_ref, *, tm, te):
    c = pl.program_id(1)
    tiles = te // PEER_NKEYS
    dn = (((1,), (1,)), ((), ()))

    @pl.when(c == 0)
    def _():
        sub = lax.broadcasted_iota(jnp.int32, (PEER_NKEYS, PEER_PICKS), 0)

        def token(t, carry):
            e = idx_ref[pl.ds(t, 1), :]
            g = gate_ref[pl.ds(t, 1), :]
            g_hi = g.astype(BF16).astype(F32)
            hit1 = sub == lax.shift_right_logical(e, 7)
            hit2 = sub == (e & (PEER_NKEYS - 1))
            m1 = jnp.where(hit1, 1.0, 0.0).astype(BF16)
            m2_hi = jnp.where(hit2, g_hi, 0.0).astype(BF16)
            m2_lo = jnp.where(hit2, g - g_hi, 0.0).astype(BF16)
            dense = lax.dot_general(jnp.concatenate([m1, m1], axis=1), jnp.concatenate([m2_hi, m2_lo], axis=1), dn,
                                    preferred_element_type=F32)
            g_ref[pl.ds(pl.multiple_of(t * GATE_PITCH, SUBLANES), PEER_NKEYS), :] = dense
            return carry

        lax.fori_loop(0, tm, token, 0, unroll=8)

    lin = lax.dot_general(hn_ref[...].astype(BF16), u_ref[...], dn, preferred_element_type=F32)
    act = jax.nn.gelu(lin)
    parts = []
    for j in range(tiles):
        gates = g_ref[pl.ds(c * tiles + j, tm, stride=GATE_PITCH), :]
        parts.append((act[:, j * PEER_NKEYS:(j + 1) * PEER_NKEYS] * gates).astype(BF16))
    ffn = jnp.dot(jnp.concatenate(parts, axis=1), v_ref[...], preferred_element_type=F32)

    @pl.when(c == 0)
    def _():
        acc_ref[...] = ffn

    @pl.when(c != 0)
    def _():
        acc_ref[...] += ffn

    @pl.when(c == pl.num_programs(1) - 1)
    def _():
        out_ref[...] = _rms(h_ref[...] + acc_ref[...], fn_ref[...])


def _peer_expert(idx, gate, hn, h, final_norm, u_bf, v_bf, tm, te):
    n = hn.shape[0]
    n_exp = u_bf.shape[0]
    row = lambda w: pl.BlockSpec((tm, w), lambda i, c: (i, 0))
    tab = pl.BlockSpec((te, D_MODEL), lambda i, c: (c, 0))
    return pl.pallas_call(
        functools.partial(_peer_expert_kernel, tm=tm, te=te),
        grid=(n // tm, n_exp // te),
        in_specs=[row(PEER_PICKS), row(PEER_PICKS), row(D_MODEL), row(D_MODEL),
                  pl.BlockSpec((1, D_MODEL), lambda i, c: (0, 0)), tab, tab],
        out_specs=row(D_MODEL),
        out_shape=jax.ShapeDtypeStruct((n, D_MODEL), F32),
        scratch_shapes=[pltpu.VMEM((tm * GATE_PITCH, PEER_NKEYS), F32), pltpu.VMEM((tm, D_MODEL), F32)],
        compiler_params=_cparams(("parallel", "arbitrary")),
        name="peer_expert",
    )(idx, gate, hn, h, final_norm, u_bf, v_bf)


def _row_tile(n, target):
    tm = min(n, target)
    while n % tm:
        tm //= 2
    return tm


def _group(x, pos, attend, wkv0, shift0, wts):
    b, t, _ = x.shape
    n = b * t
    x2 = x.reshape(n, D_MODEL)
    tm = _row_tile(n, 256)
    cos, sin = _rope_tables(jnp.tile(pos, b))
    q, k, v, p_rw = _in_proj(x2, wts["attn_norm"], wts["w_in"], cos, sin, tm)

    o_da = attend(q.reshape(b, t, DA_QK), k.reshape(b, t, DA_QK), v.reshape(b, t, DA_WIDTH))

    rows = _row_tile(t, 256)
    p3 = p_rw.reshape(n // rows, rows, RW_PROJ)
    p_bt = p_rw.reshape(b, t, RW_PROJ)
    prev_rows = p_bt[:, rows - 1::rows][:, :t // rows - 1]
    first = jnp.concatenate([shift0[:, None], prev_rows], axis=1).reshape(n // rows, 1, RW_PROJ)
    gb = _row_tile(n // rows, max(1, 256 // rows))
    r, w, k_mod, v_rw, kk, bb_, g, bonus = _rw_pre(
        p3, first, wts["rw_mu"], wts["rw_wcat"], wts["rw_w0"], wts["rw_a0"],
        wts["rw_k_k"], wts["rw_k_a"], wts["rw_r_k"], wts["seg"], gb)
    slices, per_block = _scan_split(b * RW_HEADS)
    heads = lambda a: a.reshape(b, t, RW_HEADS, RW_HD)
    key = lambda a: _key_cols(heads(a), slices, per_block)
    y_cols, h_cols = _rw_scan(key(r), key(w), key(k_mod), _value_cols(heads(v_rw), slices, per_block),
                              key(kk), key(bb_), _state_cols(wkv0, slices, per_block), tc=_row_tile(t, 64))
    y = _value_cols_inv(y_cols, b, RW_HEADS, slices, per_block)
    wkv = _state_cols_inv(h_cols, b, RW_HEADS, slices, per_block)

    h, hn, qp = _out_proj(x2, o_da.reshape(n, DA_WIDTH), y.reshape(n, RW_WIDTH), bonus, g,
                          wts["rw_ln_w"], wts["rw_ln_b"], wts["seg"], wts["w_out"], wts["ffn_norm"],
                          wts["peer_wq"], tm)
    idx, gate = _peer_topk(qp, wts["peer_keys"], tm)
    out = _peer_expert(idx, gate, hn, h, wts["final_norm"], wts["peer_u"], wts["peer_v"], tm, te=2048)
    return (out.reshape(b, t, D_MODEL), k.reshape(b, t, DA_HEADS, 2 * DA_HD), v.reshape(b, t, DA_HEADS, DA_VD),
            wkv, p_bt[:, -1])


def kernel(x_prompt, x_sample, cache_k, cache_v, state_wkv, state_shift, page_table, attn_norm, w_in, w_out,
           da_lambda_q1, da_lambda_k1, da_lambda_q2, da_lambda_k2, da_subln, rw_mu, rw_w0, rw_w_up, rw_a0,
           rw_a_up, rw_g_up, rw_k_k, rw_k_a, rw_r_k, rw_ln_w, rw_ln_b, ffn_norm, peer_wq, peer_sub_keys,
           peer_u, peer_v, final_norm):
    assert w_in.shape[0] == 1, "single-layer trunk"
    b, t_p, _ = x_prompt.shape
    bd, t_s, _ = x_sample.shape
    past = page_table.shape[1] * PAGE_SIZE
    vec = lambda a: a.reshape(1, -1).astype(F32)

    wcat = jnp.zeros((RW_LORA, 3 * RW_WIDTH), F32)
    wcat = wcat.at[:RW_W_RANK, :RW_WIDTH].set(rw_w_up[0])
    wcat = wcat.at[RW_W_RANK:RW_W_RANK + RW_A_RANK, RW_WIDTH:2 * RW_WIDTH].set(rw_a_up[0])
    wcat = wcat.at[RW_W_RANK + RW_A_RANK:, 2 * RW_WIDTH:].set(rw_g_up[0])
    lane_head = jnp.arange(RW_WIDTH) // RW_HD
    wts = dict(
        attn_norm=vec(attn_norm[0]), w_in=w_in[0].astype(BF16), w_out=w_out[0].astype(BF16),
        rw_mu=vec(rw_mu[0]), rw_wcat=wcat.astype(BF16), rw_w0=vec(rw_w0[0]), rw_a0=vec(rw_a0[0]),
        rw_k_k=vec(rw_k_k[0]), rw_k_a=vec(rw_k_a[0]), rw_r_k=vec(rw_r_k[0]),
        rw_ln_w=vec(rw_ln_w[0]), rw_ln_b=vec(rw_ln_b[0]),
        seg=(lane_head[:, None] == lane_head[None, :]).astype(F32),
        ffn_norm=vec(ffn_norm[0]), peer_wq=peer_wq[0].astype(BF16),
        peer_keys=peer_sub_keys[0].reshape(PEER_HEADS * 2, PEER_NKEYS, PEER_KD).astype(BF16),
        peer_u=peer_u.reshape(-1, D_MODEL).astype(BF16), peer_v=peer_v.reshape(-1, D_MODEL).astype(BF16),
        final_norm=vec(final_norm),
    )
    lams = (vec(da_lambda_q1[0]), vec(da_lambda_k1[0]), vec(da_lambda_q2[0]), vec(da_lambda_k2[0]))
    subln = vec(da_subln[0])

    att_p = lambda q, k, v: _attn_prompt(q, k, v, lams, subln, _row_tile(t_p, 512))
    ck = cache_k.reshape(-1, DA_VD)
    cv = cache_v.reshape(-1, DA_VD)
    att_s = lambda q, k, v: _attn_sample(q, k, v, ck, cv, page_table, lams, subln)

    pos_p = jnp.arange(t_p, dtype=jnp.int32)
    pos_s = past + jnp.arange(t_s, dtype=jnp.int32)
    yp, kp, vp, wp, sp = _group(x_prompt, pos_p, att_p, jnp.zeros((b, RW_HEADS, RW_HD, RW_HD), F32),
                                jnp.zeros((b, RW_PROJ), F32), wts)
    ys, ks, vs, ws, ss = _group(x_sample, pos_s, att_s, state_wkv[0], state_shift[0], wts)
    return (yp, ys, kp[None], vp[None], wp[None], sp[None], ks[None], vs[None], ws[None], ss[None])
```

```python
import functools
import math

import jax
import jax.numpy as jnp
from jax import lax
from jax.experimental import pallas as pl
from jax.experimental.pallas import tpu as pltpu

F32 = jnp.float32
BF16 = jnp.bfloat16

D_MODEL = 1024
PAGE_SIZE = 128
DA_HEADS = 4
DA_HD = 64
DA_VD = 2 * DA_HD
DA_WIDTH = DA_HEADS * DA_VD
DA_QK = DA_HEADS * 2 * DA_HD
RW_HEADS = 8
RW_HD = 64
RW_WIDTH = RW_HEADS * RW_HD
RW_W_RANK = 64
RW_A_RANK = 64
RW_G_RANK = 128
RW_LORA = RW_W_RANK + RW_A_RANK + RW_G_RANK
RW_PROJ = 3 * RW_WIDTH + RW_LORA
RW_GN_EPS = 64e-5
IN_PROJ = 2 * DA_QK + DA_WIDTH + RW_PROJ
PEER_HEADS = 8
PEER_NKEYS = 128
PEER_KD = 128
PEER_TOPK = 16
PEER_PICKS = PEER_HEADS * PEER_TOPK
PEER_QW = PEER_HEADS * 2 * PEER_KD
ROPE_THETA = 10000.0
NORM_EPS = 1e-6
NEG_INF = -1e30
LAM_INIT = 0.8 - 0.6 * math.exp(-0.3 * 0)

LANES = 128
SUBLANES = 8
VMEM_LIMIT_BYTES = 56 * 1024 * 1024

HIGHEST = lax.Precision.HIGHEST


def _cparams(sem):
    return pltpu.CompilerParams(dimension_semantics=sem, vmem_limit_bytes=VMEM_LIMIT_BYTES)


def _full(shape):
    return pl.BlockSpec(shape, lambda *_: (0,) * len(shape))


def _rms(x, g):
    return x * lax.rsqrt(jnp.mean(x * x, axis=-1, keepdims=True) + NORM_EPS) * g


def _sigmoid(x):
    return 1.0 / (1.0 + jnp.exp(-x))


def _rope_slab(t, cos, sin_signed):
    lane = lax.broadcasted_iota(jnp.int32, t.shape, 1)
    swapped = jnp.where(lane % DA_HD < DA_HD // 2,
                        pltpu.roll(t, LANES - DA_HD // 2, 1),
                        pltpu.roll(t, DA_HD // 2, 1))
    return t * cos + swapped * sin_signed


def _in_proj_kernel(x_ref, g_ref, w_ref, cos_ref, sin_ref, q_ref, k_ref, v_ref, p_ref):
    xn = _rms(x_ref[...], g_ref[...])
    proj = jnp.dot(xn.astype(BF16), w_ref[...], preferred_element_type=F32)
    cos = cos_ref[...]
    sin = sin_ref[...]
    for s in range(DA_QK // LANES):
        lo = s * LANES
        q_ref[:, lo:lo + LANES] = _rope_slab(proj[:, lo:lo + LANES], cos, sin)
        k_ref[:, lo:lo + LANES] = _rope_slab(proj[:, DA_QK + lo:DA_QK + lo + LANES], cos, sin)
    v_ref[...] = proj[:, 2 * DA_QK:2 * DA_QK + DA_WIDTH]
    p_ref[...] = proj[:, 2 * DA_QK + DA_WIDTH:]


def _in_proj(x, g, w_bf, cos, sin, tm):
    n = x.shape[0]
    row = lambda w: pl.BlockSpec((tm, w), lambda i: (i, 0))
    return pl.pallas_call(
        _in_proj_kernel,
        grid=(n // tm,),
        in_specs=[row(D_MODEL), _full((1, D_MODEL)), _full((D_MODEL, IN_PROJ)), row(LANES), row(LANES)],
        out_specs=[row(DA_QK), row(DA_QK), row(DA_WIDTH), row(RW_PROJ)],
        out_shape=[jax.ShapeDtypeStruct((n, DA_QK), F32), jax.ShapeDtypeStruct((n, DA_QK), F32),
                   jax.ShapeDtypeStruct((n, DA_WIDTH), F32), jax.ShapeDtypeStruct((n, RW_PROJ), F32)],
        compiler_params=_cparams(("parallel",)),
        name="in_proj",
    )(x, g, w_bf, cos, sin)


def _rope_tables(pos):
    half = DA_HD // 2
    inv = ROPE_THETA ** (-jnp.arange(half, dtype=F32) / half)
    ang = pos.astype(F32)[:, None] * inv[None, :]
    cos, sin = jnp.cos(ang), jnp.sin(ang)
    cos = jnp.tile(cos, (1, LANES // half))
    sin = jnp.tile(jnp.concatenate([-sin, sin], axis=1), (1, LANES // DA_HD))
    return cos, sin


def _lambda(lq1, lk1, lq2, lk2):
    return (jnp.exp(jnp.sum(lq1 * lk1, axis=-1, keepdims=True))
            - jnp.exp(jnp.sum(lq2 * lk2, axis=-1, keepdims=True)) + LAM_INIT)


def _attn_prompt_kernel(qi_ref, ki_ref, q_ref, k_ref, v_ref, lq1, lk1, lq2, lk2, g_ref, o_ref,
                        m1, l1, a1, m2, l2, a2, *, tq):
    qi = qi_ref[pl.program_id(2)]
    ki = ki_ref[pl.program_id(2)]
    maps = ((m1, l1, a1), (m2, l2, a2))

    @pl.when(ki == 0)
    def _():
        for m, l, a in maps:
            m[...] = jnp.full(m.shape, NEG_INF, F32)
            l[...] = jnp.zeros(l.shape, F32)
            a[...] = jnp.zeros(a.shape, F32)

    def block(diagonal):
        q = q_ref[0] * (DA_HD ** -0.5)
        lane = lax.broadcasted_iota(jnp.int32, q.shape, 1)
        q1 = jnp.where(lane < DA_HD, q, 0.0).astype(BF16)
        q2 = jnp.where(lane < DA_HD, 0.0, q).astype(BF16)
        kb = k_ref[0].astype(BF16)
        vt = v_ref[0].T.astype(BF16)
        dn = (((1,), (1,)), ((), ()))
        state = [(m[...], l[...], a[...]) for m, l, a in maps]
        new = []
        for qm, (m_old, l_old, a_old) in zip((q1, q2), state):
            s = lax.dot_general(kb, qm, dn, preferred_element_type=F32)
            if diagonal:
                key = lax.broadcasted_iota(jnp.int32, (tq, tq), 0)
                qry = lax.broadcasted_iota(jnp.int32, (tq, tq), 1)
                s = jnp.where(key <= qry, s, NEG_INF)
            m_new = jnp.maximum(m_old, jnp.max(s, axis=0, keepdims=True))
            alpha = jnp.exp(m_old - m_new)
            p = jnp.exp(s - m_new)
            new.append((m_new, alpha * l_old + jnp.sum(p, axis=0, keepdims=True),
                        alpha * a_old + jnp.dot(vt, p.astype(BF16), preferred_element_type=F32)))
        for (m, l, a), (m_new, l_new, a_new) in zip(maps, new):
            m[...] = m_new
            l[...] = l_new
            a[...] = a_new

    @pl.when(ki < qi)
    def _():
        block(diagonal=False)

    @pl.when(ki == qi)
    def _():
        block(diagonal=True)
        lam = _lambda(lq1[...], lk1[...], lq2[...], lk2[...])
        o = (a1[...] / l1[...] - lam * (a2[...] / l2[...])).T
        o_ref[0] = _rms(o, g_ref[...]) * (1.0 - LAM_INIT)


def _attn_prompt(q, k, v, lams, subln, tq):
    b, t, _ = q.shape
    nq = t // tq
    pairs = [(i, j) for i in range(nq) for j in range(i + 1)]
    qi_tab = jnp.asarray([i for i, _ in pairs], jnp.int32)
    ki_tab = jnp.asarray([j for _, j in pairs], jnp.int32)
    qspec = pl.BlockSpec((1, tq, DA_VD), lambda bi, h, p, qi, ki: (bi, qi[p], h))
    kspec = pl.BlockSpec((1, tq, DA_VD), lambda bi, h, p, qi, ki: (bi, ki[p], h))
    vec = lambda w: pl.BlockSpec((1, w), lambda bi, h, p, qi, ki: (0, 0))
    grid_spec = pltpu.PrefetchScalarGridSpec(
        num_scalar_prefetch=2,
        grid=(b, DA_HEADS, len(pairs)),
        in_specs=[qspec, kspec, kspec, vec(DA_HD), vec(DA_HD), vec(DA_HD), vec(DA_HD), vec(DA_VD)],
        out_specs=qspec,
        scratch_shapes=[pltpu.VMEM((1, tq), F32), pltpu.VMEM((1, tq), F32), pltpu.VMEM((DA_VD, tq), F32),
                        pltpu.VMEM((1, tq), F32), pltpu.VMEM((1, tq), F32), pltpu.VMEM((DA_VD, tq), F32)],
    )
    return pl.pallas_call(
        functools.partial(_attn_prompt_kernel, tq=tq),
        grid_spec=grid_spec,
        out_shape=jax.ShapeDtypeStruct((b, t, DA_WIDTH), F32),
        compiler_params=_cparams(("parallel", "parallel", "arbitrary")),
        name="attn_prompt",
    )(qi_tab, ki_tab, q, k, v, *lams, subln)


def _attn_sample_kernel(pt_ref, q_ref, *refs, n_steps, pps, t_new):
    ck_refs, cv_refs = refs[:pps], refs[pps:2 * pps]
    kn_ref, vn_ref, lq1, lk1, lq2, lk2, g_ref, o_ref, qb_ref, m_ref, l_ref, acc_ref = refs[2 * pps:]
    p = pl.program_id(1)
    dn = (((1,), (1,)), ((), ()))

    @pl.when(p == 0)
    def _():
        q = q_ref[0] * (DA_HD ** -0.5)
        lane = lax.broadcasted_iota(jnp.int32, (t_new, DA_VD), 1)
        for h in range(DA_HEADS):
            slab = q[:, h * DA_VD:(h + 1) * DA_VD]
            qb_ref[h, 0:t_new, :] = jnp.where(lane < DA_HD, slab, 0.0)
            qb_ref[h, t_new:2 * t_new, :] = jnp.where(lane < DA_HD, 0.0, slab)
        m_ref[...] = jnp.full(m_ref.shape, NEG_INF, F32)
        l_ref[...] = jnp.zeros(l_ref.shape, F32)
        acc_ref[...] = jnp.zeros(acc_ref.shape, F32)

    @pl.when(p < n_steps)
    def _():
        head_rows = [pl.ds(h, PAGE_SIZE, stride=DA_HEADS) for h in range(DA_HEADS)]
        m_old = [m_ref[h] for h in range(DA_HEADS)]
        l_old = [l_ref[h] for h in range(DA_HEADS)]
        acc_old = [acc_ref[h] for h in range(DA_HEADS)]
        qb = [qb_ref[h].astype(BF16) for h in range(DA_HEADS)]
        s = [jnp.concatenate([lax.dot_general(qb[h], ck[head_rows[h], :].astype(BF16), dn, preferred_element_type=F32)
                              for ck in ck_refs], axis=1) for h in range(DA_HEADS)]
        m_new = [jnp.maximum(m_old[h], jnp.max(s[h], axis=-1, keepdims=True)) for h in range(DA_HEADS)]
        alpha = [jnp.exp(m_old[h] - m_new[h]) for h in range(DA_HEADS)]
        pr = [jnp.exp(s[h] - m_new[h]) for h in range(DA_HEADS)]
        l_new = [alpha[h] * l_old[h] + jnp.sum(pr[h], axis=-1, keepdims=True) for h in range(DA_HEADS)]
        acc_new = []
        for h in range(DA_HEADS):
            pv = None
            for i, cv in enumerate(cv_refs):
                term = jnp.dot(pr[h][:, i * PAGE_SIZE:(i + 1) * PAGE_SIZE].astype(BF16),
                               cv[head_rows[h], :].astype(BF16), preferred_element_type=F32)
                pv = term if pv is None else pv + term
            acc_new.append(alpha[h] * acc_old[h] + pv)
        for h in range(DA_HEADS):
            m_ref[h] = m_new[h]
            l_ref[h] = l_new[h]
            acc_ref[h] = acc_new[h]

    @pl.when(p == n_steps)
    def _():
        lam = _lambda(lq1[...], lk1[...], lq2[...], lk2[...])
        t_of_row = lax.broadcasted_iota(jnp.int32, (2 * t_new, 1), 0) % t_new
        for h in range(DA_HEADS):
            qb = qb_ref[h]
            kn = kn_ref[0, :, h * DA_VD:(h + 1) * DA_VD]
            vn = vn_ref[0, :, h * DA_VD:(h + 1) * DA_VD]
            s_new = [jnp.where(t_of_row >= j, jnp.sum(qb * kn[j:j + 1, :], axis=-1, keepdims=True), NEG_INF)
                     for j in range(t_new)]
            m_old = m_ref[h]
            m_new = m_old
            for s_j in s_new:
                m_new = jnp.maximum(m_new, s_j)
            alpha = jnp.exp(m_old - m_new)
            l = alpha * l_ref[h]
            acc = alpha * acc_ref[h]
            for j, s_j in enumerate(s_new):
                p_j = jnp.exp(s_j - m_new)
                l = l + p_j
                acc = acc + p_j * vn[j:j + 1, :]
            o_all = acc / l
            o = o_all[0:t_new] - lam * o_all[t_new:2 * t_new]
            o_ref[0, :, h * DA_VD:(h + 1) * DA_VD] = _rms(o, g_ref[...]) * (1.0 - LAM_INIT)


def _attn_sample(q, k_new, v_new, cache_k, cache_v, page_table, lams, subln):
    bd, t_new, _ = q.shape
    n_pages = page_table.shape[1]
    pps = next(c for c in (8, 4, 2, 1) if n_pages % c == 0)
    n_steps = n_pages // pps
    page_rows = PAGE_SIZE * DA_HEADS
    tok = pl.BlockSpec((1, t_new, DA_WIDTH), lambda b, p, pt: (b, 0, 0))

    def page(i):
        return pl.BlockSpec((page_rows, DA_VD),
                            lambda b, p, pt: (pt[b, jnp.minimum(p, n_steps - 1) * pps + i], 0))

    pages = [page(i) for i in range(pps)]
    vec = lambda w: pl.BlockSpec((1, w), lambda b, p, pt: (0, 0))
    rows = 2 * t_new
    grid_spec = pltpu.PrefetchScalarGridSpec(
        num_scalar_prefetch=1,
        grid=(bd, n_steps + 1),
        in_specs=[tok] + pages + pages + [tok, tok, vec(DA_HD), vec(DA_HD), vec(DA_HD), vec(DA_HD), vec(DA_VD)],
        out_specs=tok,
        scratch_shapes=[pltpu.VMEM((DA_HEADS, rows, DA_VD), F32), pltpu.VMEM((DA_HEADS, rows, 1), F32),
                        pltpu.VMEM((DA_HEADS, rows, 1), F32), pltpu.VMEM((DA_HEADS, rows, DA_VD), F32)],
    )
    return pl.pallas_call(
        functools.partial(_attn_sample_kernel, n_steps=n_steps, pps=pps, t_new=t_new),
        grid_spec=grid_spec,
        out_shape=jax.ShapeDtypeStruct((bd, t_new, DA_WIDTH), F32),
        compiler_params=_cparams(("parallel", "arbitrary")),
        name="attn_sample",
    )(page_table, q, *([cache_k] * pps), *([cache_v] * pps), k_new, v_new, *lams, subln)


def _head_sum(x, seg):
    return jnp.dot(x, seg, preferred_element_type=F32, precision=HIGHEST)


def _rw_pre_kernel(p_ref, first_ref, mu_ref, wcat_ref, w0_ref, a0_ref, kk_ref, ka_ref, rk_ref, seg_ref,
                   r_out, w_out, k_out, v_out, kk_out, b_out, g_out, bonus_out):
    p = p_ref[...]
    gb, rows, _ = p.shape
    row = lax.broadcasted_iota(jnp.int32, p.shape, 1)
    p_prev = jnp.where(row == 0, first_ref[...], pltpu.roll(p, 1, 1))
    ps = (p + (p_prev - p) * mu_ref[...]).reshape(gb * rows, RW_PROJ)
    w_ = RW_WIDTH
    r, k, v = ps[:, :w_], ps[:, w_:2 * w_], ps[:, 2 * w_:3 * w_]
    x = ps[:, 3 * w_:]
    lane = lax.broadcasted_iota(jnp.int32, x.shape, 1)
    act = jnp.where(lane < RW_W_RANK, jnp.tanh(x), jnp.where(lane < RW_W_RANK + RW_A_RANK, x, _sigmoid(x)))
    lora = jnp.dot(act.astype(BF16), wcat_ref[...], preferred_element_type=F32)
    decay = jnp.exp(-math.exp(-0.5) * _sigmoid(w0_ref[...] + lora[:, :w_]))
    a = _sigmoid(a0_ref[...] + lora[:, w_:2 * w_])
    seg = seg_ref[...]
    kk = k * kk_ref[...]
    kk = kk / jnp.maximum(jnp.sqrt(_head_sum(kk * kk, seg)), 1e-12)
    k_mod = k * (1.0 + (a - 1.0) * ka_ref[...])
    r_out[...] = r
    w_out[...] = decay
    k_out[...] = k_mod
    v_out[...] = v
    kk_out[...] = kk
    b_out[...] = kk * a
    g_out[...] = lora[:, 2 * w_:]
    bonus_out[...] = _head_sum(r * k_mod * rk_ref[...], seg) * v


def _rw_pre(p3, first, mu, wcat, w0, a0, k_k, k_a, r_k, seg, gb):
    g, rows, _ = p3.shape
    n = g * rows
    vec = lambda w: _full((1, w))
    out = pl.BlockSpec((gb * rows, RW_WIDTH), lambda i: (i, 0))
    return pl.pallas_call(
        _rw_pre_kernel,
        grid=(g // gb,),
        in_specs=[pl.BlockSpec((gb, rows, RW_PROJ), lambda i: (i, 0, 0)),
                  pl.BlockSpec((gb, 1, RW_PROJ), lambda i: (i, 0, 0)),
                  vec(RW_PROJ), _full((RW_LORA, 3 * RW_WIDTH)), vec(RW_WIDTH), vec(RW_WIDTH),
                  vec(RW_WIDTH), vec(RW_WIDTH), vec(RW_WIDTH), _full((RW_WIDTH, RW_WIDTH))],
        out_specs=[out] * 8,
        out_shape=[jax.ShapeDtypeStruct((n, RW_WIDTH), F32)] * 8,
        compiler_params=_cparams(("parallel",)),
        name="rw_pre",
    )(p3, first, mu, wcat, w0, a0, k_k, k_a, r_k, seg)


def _rw_scan_kernel(r_ref, w_ref, k_ref, v_ref, kk_ref, b_ref, h0_ref, y_ref, h_ref, *, tc):
    @pl.when(pl.program_id(1) == 0)
    def _():
        h_ref[...] = h0_ref[...]

    def key_row(ref, t, j):
        return ref[0, t, pl.ds(j, 1), :]

    def step(t, carry):
        parts = [None] * 4
        for j in range(RW_HD):
            term = h_ref[0, j] * key_row(kk_ref, t, j)
            parts[j % 4] = term if parts[j % 4] is None else parts[j % 4] + term
        sa = -((parts[0] + parts[1]) + (parts[2] + parts[3]))
        v = v_ref[0, t]
        parts = [None] * 4
        for j in range(RW_HD):
            h = h_ref[0, j] * key_row(w_ref, t, j) + sa * key_row(b_ref, t, j) + v * key_row(k_ref, t, j)
            h_ref[0, j] = h
            term = h * key_row(r_ref, t, j)
            parts[j % 4] = term if parts[j % 4] is None else parts[j % 4] + term
        y_ref[0, t] = (parts[0] + parts[1]) + (parts[2] + parts[3])
        return carry

    lax.fori_loop(0, tc, step, 0)


def _rw_scan(r, w, k, v, kk, b, h0, tc):
    nb, t, rows = v.shape[0], v.shape[1], v.shape[2]
    key = pl.BlockSpec((1, tc, RW_HD, LANES), lambda i, c: (i, c, 0, 0))
    val = pl.BlockSpec((1, tc, rows, LANES), lambda i, c: (i, c, 0, 0))
    st = pl.BlockSpec((1, RW_HD, rows, LANES), lambda i, c: (i, 0, 0, 0))
    return pl.pallas_call(
        functools.partial(_rw_scan_kernel, tc=tc),
        grid=(nb, t // tc),
        in_specs=[key, key, key, val, key, key, st],
        out_specs=[val, st],
        out_shape=[jax.ShapeDtypeStruct((nb, t, rows, LANES), F32),
                   jax.ShapeDtypeStruct((nb, RW_HD, rows, LANES), F32)],
        compiler_params=_cparams(("parallel", "arbitrary")),
        name="rw_scan",
    )(r, w, k, v, kk, b, h0)


def _scan_split(n_states):
    slices = max(1, LANES // n_states)
    assert (n_states * slices) % LANES == 0 and RW_HD % slices == 0 and (RW_HD // slices) % SUBLANES == 0, n_states
    return slices, LANES // slices


def _key_cols(x, slices, per_block):
    b, t, h, j = x.shape
    x = x.transpose(1, 3, 0, 2).reshape(t, j, (b * h) // per_block, 1, per_block)
    x = jnp.broadcast_to(x, (t, j, (b * h) // per_block, slices, per_block))
    return x.transpose(2, 0, 1, 3, 4).reshape((b * h) // per_block, t, j, LANES)


def _value_cols(x, slices, per_block):
    b, t, h, i = x.shape
    rows = i // slices
    x = x.reshape(b, t, h, slices, rows).transpose(1, 4, 3, 0, 2).reshape(t, rows, slices, (b * h) // per_block, per_block)
    return x.transpose(3, 0, 1, 2, 4).reshape((b * h) // per_block, t, rows, LANES)


def _value_cols_inv(y, b, h, slices, per_block):
    nb, t, rows, _ = y.shape
    y = y.reshape(nb, t, rows, slices, per_block).transpose(1, 2, 3, 0, 4).reshape(t, rows, slices, b, h)
    return y.transpose(3, 0, 4, 2, 1).reshape(b, t, h, slices * rows)


def _state_cols(s, slices, per_block):
    b, h, i, j = s.shape
    rows = i // slices
    s = s.reshape(b, h, slices, rows, j).transpose(4, 3, 2, 0, 1).reshape(j, rows, slices, (b * h) // per_block, per_block)
    return s.transpose(3, 0, 1, 2, 4).reshape((b * h) // per_block, j, rows, LANES)


def _state_cols_inv(hs, b, h, slices, per_block):
    nb, j, rows, _ = hs.shape
    hs = hs.reshape(nb, j, rows, slices, per_block).transpose(1, 2, 3, 0, 4).reshape(j, rows, slices, b, h)
    return hs.transpose(3, 4, 2, 1, 0).reshape(b, h, slices * rows, j)


def _out_proj_kernel(x_ref, oda_ref, y_ref, bonus_ref, g_ref, lnw_ref, lnb_ref, seg_ref, wo_ref, fn_ref, wq_ref,
                     h_ref, hn_ref, qp_ref):
    seg = seg_ref[...]
    y = y_ref[...]
    d = y - _head_sum(y, seg) * (1.0 / RW_HD)
    var = _head_sum(d * d, seg) * (1.0 / RW_HD)
    yn = d * lax.rsqrt(var + RW_GN_EPS) * lnw_ref[...] + lnb_ref[...]
    o_rw = (yn + bonus_ref[...]) * g_ref[...]
    h = (x_ref[...]
         + jnp.dot(oda_ref[...].astype(BF16), wo_ref[:DA_WIDTH, :], preferred_element_type=F32)
         + jnp.dot(o_rw.astype(BF16), wo_ref[DA_WIDTH:, :], preferred_element_type=F32))
    hn = _rms(h, fn_ref[...])
    h_ref[...] = h
    hn_bf = hn.astype(BF16)
    hn_ref[...] = hn_bf
    qp_ref[...] = jnp.dot(hn_bf, wq_ref[...], preferred_element_type=F32).astype(BF16)


def _out_proj(x, o_da, y, bonus, g, ln_w, ln_b, seg, w_out_bf, ffn_norm, wq_bf, tm):
    n = x.shape[0]
    row = lambda w: pl.BlockSpec((tm, w), lambda i: (i, 0))
    vec = lambda w: _full((1, w))
    return pl.pallas_call(
        _out_proj_kernel,
        grid=(n // tm,),
        in_specs=[row(D_MODEL), row(DA_WIDTH), row(RW_WIDTH), row(RW_WIDTH), row(RW_WIDTH),
                  vec(RW_WIDTH), vec(RW_WIDTH), _full((RW_WIDTH, RW_WIDTH)),
                  _full((DA_WIDTH + RW_WIDTH, D_MODEL)), vec(D_MODEL), _full((D_MODEL, PEER_QW))],
        out_specs=[row(D_MODEL), row(D_MODEL), row(PEER_QW)],
        out_shape=[jax.ShapeDtypeStruct((n, D_MODEL), F32), jax.ShapeDtypeStruct((n, D_MODEL), BF16),
                   jax.ShapeDtypeStruct((n, PEER_QW), BF16)],
        compiler_params=_cparams(("parallel",)),
        name="out_proj",
    )(x, o_da, y, bonus, g, ln_w, ln_b, seg, w_out_bf, ffn_norm, wq_bf)


def _top_k_rows(s, order, payload, k):
    vals, pays = [], []
    for _ in range(k):
        m = jnp.max(s, axis=0, keepdims=True)
        first = jnp.min(jnp.where(s == m, order, 1e9), axis=0, keepdims=True)
        hit = order == first
        vals.append(m)
        pays.append(first if payload is None else jnp.max(jnp.where(hit, payload, -1.0), axis=0, keepdims=True))
        s = jnp.where(hit, -jnp.inf, s)
    return jnp.concatenate(vals, axis=0), jnp.concatenate(pays, axis=0)


_PAIR_BLOCKS = ([("a", a0, 0) for a0 in (0, 8)] + [("a", 0, b) for b in range(1, 8)] + [("b", 0, 8)])


def _pair_candidates(sv, si, tm):
    r8 = lax.broadcasted_iota(jnp.int32, (SUBLANES, tm), 0)
    cand, flat, cidx = [], [], []
    for kind, a0, b0 in _PAIR_BLOCKS:
        if kind == "a":
            a, b = r8 + a0, jnp.full((SUBLANES, tm), b0, jnp.int32)
            val = sv[0][a0:a0 + SUBLANES] + sv[1][b0:b0 + 1]
            idx = si[0][a0:a0 + SUBLANES] * PEER_NKEYS + si[1][b0:b0 + 1]
        else:
            a, b = jnp.full((SUBLANES, tm), a0, jnp.int32), r8 + b0
            val = sv[0][a0:a0 + 1] + sv[1][b0:b0 + SUBLANES]
            idx = si[0][a0:a0 + 1] * PEER_NKEYS + si[1][b0:b0 + SUBLANES]
        ok = (a + 1) * (b + 1) <= PEER_TOPK
        cand.append(jnp.where(ok, val, -jnp.inf))
        flat.append(jnp.where(ok, a * PEER_TOPK + b, 1000000 + a * PEER_TOPK + b).astype(F32))
        cidx.append(idx)
    return jnp.concatenate(cand, axis=0), jnp.concatenate(flat, axis=0), jnp.concatenate(cidx, axis=0)


def _peer_topk_kernel(qp_ref, keys_ref, idx_ref, gate_ref, *, tm):
    key_row = lax.broadcasted_iota(jnp.int32, (PEER_NKEYS, tm), 0).astype(F32)
    dn = (((1,), (1,)), ((), ()))
    idx_rows, gate_rows = [], []
    for h in range(PEER_HEADS):
        sv, si = [], []
        for c in range(2):
            hc = 2 * h + c
            q = qp_ref[:, hc * PEER_KD:(hc + 1) * PEER_KD]
            s = lax.dot_general(keys_ref[hc], q, dn, preferred_element_type=F32)
            v_, i_ = _top_k_rows(s, key_row, None, PEER_TOPK)
            sv.append(v_)
            si.append(i_)
        cand, flat, cidx = _pair_candidates(sv, si, tm)
        bv, eidx = _top_k_rows(cand, flat, cidx, PEER_TOPK)
        e = jnp.exp(bv - bv[0:1, :])
        gate_rows.append(e / jnp.sum(e, axis=0, keepdims=True))
        idx_rows.append(eidx)
    idx_ref[...] = jnp.concatenate(idx_rows, axis=0).T.astype(jnp.int32)
    gate_ref[...] = jnp.concatenate(gate_rows, axis=0).T


def _peer_topk(qp, keys_bf, tm):
    n = qp.shape[0]
    return pl.pallas_call(
        functools.partial(_peer_topk_kernel, tm=tm),
        grid=(n // tm,),
        in_specs=[pl.BlockSpec((tm, PEER_QW), lambda i: (i, 0)),
                  _full((PEER_HEADS * 2, PEER_NKEYS, PEER_KD))],
        out_specs=[pl.BlockSpec((tm, PEER_PICKS), lambda i: (i, 0))] * 2,
        out_shape=[jax.ShapeDtypeStruct((n, PEER_PICKS), jnp.int32),
                   jax.ShapeDtypeStruct((n, PEER_PICKS), F32)],
        compiler_params=_cparams(("parallel",)),
        name="peer_topk",
    )(qp, keys_bf)


GATE_PITCH = PEER_NKEYS + SUBLANES


def _peer_expert_kernel(idx_ref, gate_ref, hn_ref, h_ref, fn_ref, u_ref, v_ref, out_ref, g_ref, acc_ref, *, tm, te):
    c = pl.program_id(1)
    tiles = te // PEER_NKEYS
    dn = (((1,), (1,)), ((), ()))

    @pl.when(c == 0)
    def _():
        sub = lax.broadcasted_iota(jnp.int32, (PEER_NKEYS, PEER_PICKS), 0)

        def token(t, carry):
            e = idx_ref[pl.ds(t, 1), :]
            g = gate_ref[pl.ds(t, 1), :]
            g_hi = g.astype(BF16).astype(F32)
            hit1 = sub == lax.shift_right_logical(e, 7)
            hit2 = sub == (e & (PEER_NKEYS - 1))
            m1 = jnp.where(hit1, 1.0, 0.0).astype(BF16)
            m2_hi = jnp.where(hit2, g_hi, 0.0).astype(BF16)
            m2_lo = jnp.where(hit2, g - g_hi, 0.0).astype(BF16)
            dense = lax.dot_general(jnp.concatenate([m1, m1], axis=1), jnp.concatenate([m2_hi, m2_lo], axis=1), dn,
                                    preferred_element_type=F32)
            g_ref[pl.ds(pl.multiple_of(t * GATE_PITCH, SUBLANES), PEER_NKEYS), :] = dense
            return carry

        lax.fori_loop(0, tm, token, 0, unroll=8)

    lin = lax.dot_general(hn_ref[...], u_ref[...], dn, preferred_element_type=F32)
    act = jax.nn.gelu(lin)
    parts = []
    for j in range(tiles):
        gates = g_ref[pl.ds(c * tiles + j, tm, stride=GATE_PITCH), :]
        parts.append((act[:, j * PEER_NKEYS:(j + 1) * PEER_NKEYS] * gates).astype(BF16))
    ffn = jnp.dot(jnp.concatenate(parts, axis=1), v_ref[...], preferred_element_type=F32)

    @pl.when(c == 0)
    def _():
        acc_ref[...] = ffn

    @pl.when(c != 0)
    def _():
        acc_ref[...] += ffn

    @pl.when(c == pl.num_programs(1) - 1)
    def _():
        out_ref[...] = _rms(h_ref[...] + acc_ref[...], fn_ref[...])


def _peer_expert(idx, gate, hn, h, final_norm, u_bf, v_bf, tm, te):
    n = hn.shape[0]
    n_exp = u_bf.shape[0]
    row = lambda w: pl.BlockSpec((tm, w), lambda i, c: (i, 0))
    tab = pl.BlockSpec((te, D_MODEL), lambda i, c: (c, 0))
    return pl.pallas_call(
        functools.partial(_peer_expert_kernel, tm=tm, te=te),
        grid=(n // tm, n_exp // te),
        in_specs=[row(PEER_PICKS), row(PEER_PICKS), row(D_MODEL), row(D_MODEL),
                  pl.BlockSpec((1, D_MODEL), lambda i, c: (0, 0)), tab, tab],
        out_specs=row(D_MODEL),
        out_shape=jax.ShapeDtypeStruct((n, D_MODEL), F32),
        scratch_shapes=[pltpu.VMEM((tm * GATE_PITCH, PEER_NKEYS), F32), pltpu.VMEM((tm, D_MODEL), F32)],
        compiler_params=_cparams(("parallel", "arbitrary")),
        name="peer_expert",
    )(idx, gate, hn, h, final_norm, u_bf, v_bf)


def _row_tile(n, target):
    tm = min(n, target)
    while n % tm:
        tm //= 2
    return tm


def _group(x, pos, attend, wkv0, shift0, wts):
    b, t, _ = x.shape
    n = b * t
    x2 = x.reshape(n, D_MODEL)
    tm = _row_tile(n, 256)
    cos, sin = _rope_tables(jnp.tile(pos, b))
    q, k, v, p_rw = _in_proj(x2, wts["attn_norm"], wts["w_in"], cos, sin, tm)

    o_da = attend(q.reshape(b, t, DA_QK), k.reshape(b, t, DA_QK), v.reshape(b, t, DA_WIDTH))

    rows = _row_tile(t, 256)
    p3 = p_rw.reshape(n // rows, rows, RW_PROJ)
    p_bt = p_rw.reshape(b, t, RW_PROJ)
    prev_rows = p_bt[:, rows - 1::rows][:, :t // rows - 1]
    first = jnp.concatenate([shift0[:, None], prev_rows], axis=1).reshape(n // rows, 1, RW_PROJ)
    gb = _row_tile(n // rows, max(1, 256 // rows))
    r, w, k_mod, v_rw, kk, bb_, g, bonus = _rw_pre(
        p3, first, wts["rw_mu"], wts["rw_wcat"], wts["rw_w0"], wts["rw_a0"],
        wts["rw_k_k"], wts["rw_k_a"], wts["rw_r_k"], wts["seg"], gb)
    slices, per_block = _scan_split(b * RW_HEADS)
    heads = lambda a: a.reshape(b, t, RW_HEADS, RW_HD)
    key = lambda a: _key_cols(heads(a), slices, per_block)
    y_cols, h_cols = _rw_scan(key(r), key(w), key(k_mod), _value_cols(heads(v_rw), slices, per_block),
                              key(kk), key(bb_), _state_cols(wkv0, slices, per_block), tc=_row_tile(t, 64))
    y = _value_cols_inv(y_cols, b, RW_HEADS, slices, per_block)
    wkv = _state_cols_inv(h_cols, b, RW_HEADS, slices, per_block)

    h, hn, qp = _out_proj(x2, o_da.reshape(n, DA_WIDTH), y.reshape(n, RW_WIDTH), bonus, g,
                          wts["rw_ln_w"], wts["rw_ln_b"], wts["seg"], wts["w_out"], wts["ffn_norm"],
                          wts["peer_wq"], tm)
    idx, gate = _peer_topk(qp, wts["peer_keys"], tm)
    out = _peer_expert(idx, gate, hn, h, wts["final_norm"], wts["peer_u"], wts["peer_v"], tm, te=2048)
    return (out.reshape(b, t, D_MODEL), k.reshape(b, t, DA_HEADS, 2 * DA_HD), v.reshape(b, t, DA_HEADS, DA_VD),
            wkv, p_bt[:, -1])


def kernel(x_prompt, x_sample, cache_k, cache_v, state_wkv, state_shift, page_table, attn_norm, w_in, w_out,
           da_lambda_q1, da_lambda_k1, da_lambda_q2, da_lambda_k2, da_subln, rw_mu, rw_w0, rw_w_up, rw_a0,
           rw_a_up, rw_g_up, rw_k_k, rw_k_a, rw_r_k, rw_ln_w, rw_ln_b, ffn_norm, peer_wq, peer_sub_keys,
           peer_u, peer_v, final_norm):
    assert w_in.shape[0] == 1, "single-layer trunk"
    b, t_p, _ = x_prompt.shape
    bd, t_s, _ = x_sample.shape
    past = page_table.shape[1] * PAGE_SIZE
    vec = lambda a: a.reshape(1, -1).astype(F32)

    wcat = jnp.zeros((RW_LORA, 3 * RW_WIDTH), F32)
    wcat = wcat.at[:RW_W_RANK, :RW_WIDTH].set(rw_w_up[0])
    wcat = wcat.at[RW_W_RANK:RW_W_RANK + RW_A_RANK, RW_WIDTH:2 * RW_WIDTH].set(rw_a_up[0])
    wcat = wcat.at[RW_W_RANK + RW_A_RANK:, 2 * RW_WIDTH:].set(rw_g_up[0])
    lane_head = jnp.arange(RW_WIDTH) // RW_HD
    wts = dict(
        attn_norm=vec(attn_norm[0]), w_in=w_in[0].astype(BF16), w_out=w_out[0].astype(BF16),
        rw_mu=vec(rw_mu[0]), rw_wcat=wcat.astype(BF16), rw_w0=vec(rw_w0[0]), rw_a0=vec(rw_a0[0]),
        rw_k_k=vec(rw_k_k[0]), rw_k_a=vec(rw_k_a[0]), rw_r_k=vec(rw_r_k[0]),
        rw_ln_w=vec(rw_ln_w[0]), rw_ln_b=vec(rw_ln_b[0]),
        seg=(lane_head[:, None] == lane_head[None, :]).astype(F32),
        ffn_norm=vec(ffn_norm[0]), peer_wq=peer_wq[0].astype(BF16),
        peer_keys=peer_sub_keys[0].reshape(PEER_HEADS * 2, PEER_NKEYS, PEER_KD).astype(BF16),
        peer_u=peer_u.reshape(-1, D_MODEL).astype(BF16), peer_v=peer_v.reshape(-1, D_MODEL).astype(BF16),
        final_norm=vec(final_norm),
    )
    lams = (vec(da_lambda_q1[0]), vec(da_lambda_k1[0]), vec(da_lambda_q2[0]), vec(da_lambda_k2[0]))
    subln = vec(da_subln[0])

    att_p = lambda q, k, v: _attn_prompt(q, k, v, lams, subln, _row_tile(t_p, 512))
    ck = cache_k.reshape(-1, DA_VD)
    cv = cache_v.reshape(-1, DA_VD)
    att_s = lambda q, k, v: _attn_sample(q, k, v, ck, cv, page_table, lams, subln)

    pos_p = jnp.arange(t_p, dtype=jnp.int32)
    pos_s = past + jnp.arange(t_s, dtype=jnp.int32)
    yp, kp, vp, wp, sp = _group(x_prompt, pos_p, att_p, jnp.zeros((b, RW_HEADS, RW_HD, RW_HD), F32),
                                jnp.zeros((b, RW_PROJ), F32), wts)
    ys, ks, vs, ws, ss = _group(x_sample, pos_s, att_s, state_wkv[0], state_shift[0], wts)
    return (yp, ys, kp[None], vp[None], wp[None], sp[None], ks[None], vs[None], ws[None], ss[None])
```

```python
import functools
import math

import jax
import jax.numpy as jnp
from jax import lax
from jax.experimental import pallas as pl
from jax.experimental.pallas import tpu as pltpu

F32 = jnp.float32
BF16 = jnp.bfloat16

D_MODEL = 1024
PAGE_SIZE = 128
DA_HEADS = 4
DA_HD = 64
DA_VD = 2 * DA_HD
DA_WIDTH = DA_HEADS * DA_VD
DA_QK = DA_HEADS * 2 * DA_HD
RW_HEADS = 8
RW_HD = 64
RW_WIDTH = RW_HEADS * RW_HD
RW_W_RANK = 64
RW_A_RANK = 64
RW_G_RANK = 128
RW_LORA = RW_W_RANK + RW_A_RANK + RW_G_RANK
RW_PROJ = 3 * RW_WIDTH + RW_LORA
RW_GN_EPS = 64e-5
IN_PROJ = 2 * DA_QK + DA_WIDTH + RW_PROJ
PEER_HEADS = 8
PEER_NKEYS = 128
PEER_KD = 128
PEER_TOPK = 16
PEER_PICKS = PEER_HEADS * PEER_TOPK
PEER_QW = PEER_HEADS * 2 * PEER_KD
ROPE_THETA = 10000.0
NORM_EPS = 1e-6
NEG_INF = -1e30
LAM_INIT = 0.8 - 0.6 * math.exp(-0.3 * 0)

LANES = 128
SUBLANES = 8
VMEM_LIMIT_BYTES = 56 * 1024 * 1024

HIGHEST = lax.Precision.HIGHEST


def _cparams(sem):
    return pltpu.CompilerParams(dimension_semantics=sem, vmem_limit_bytes=VMEM_LIMIT_BYTES)


def _full(shape):
    return pl.BlockSpec(shape, lambda *_: (0,) * len(shape))


def _rms(x, g):
    return x * lax.rsqrt(jnp.mean(x * x, axis=-1, keepdims=True) + NORM_EPS) * g


def _sigmoid(x):
    return 1.0 / (1.0 + jnp.exp(-x))


def _rope_slab(t, cos, sin_signed):
    lane = lax.broadcasted_iota(jnp.int32, t.shape, 1)
    swapped = jnp.where(lane % DA_HD < DA_HD // 2,
                        pltpu.roll(t, LANES - DA_HD // 2, 1),
                        pltpu.roll(t, DA_HD // 2, 1))
    return t * cos + swapped * sin_signed


def _in_proj_kernel(x_ref, g_ref, w_ref, cos_ref, sin_ref, q_ref, k_ref, v_ref, p_ref):
    xn = _rms(x_ref[...], g_ref[...])
    proj = jnp.dot(xn.astype(BF16), w_ref[...], preferred_element_type=F32)
    cos = cos_ref[...]
    sin = sin_ref[...]
    for s in range(DA_QK // LANES):
        lo = s * LANES
        q_ref[:, lo:lo + LANES] = _rope_slab(proj[:, lo:lo + LANES], cos, sin)
        k_ref[:, lo:lo + LANES] = _rope_slab(proj[:, DA_QK + lo:DA_QK + lo + LANES], cos, sin)
    v_ref[...] = proj[:, 2 * DA_QK:2 * DA_QK + DA_WIDTH]
    p_ref[...] = proj[:, 2 * DA_QK + DA_WIDTH:]


def _in_proj(x, g, w_bf, cos, sin, tm):
    n = x.shape[0]
    row = lambda w: pl.BlockSpec((tm, w), lambda i: (i, 0))
    return pl.pallas_call(
        _in_proj_kernel,
        grid=(n // tm,),
        in_specs=[row(D_MODEL), _full((1, D_MODEL)), _full((D_MODEL, IN_PROJ)), row(LANES), row(LANES)],
        out_specs=[row(DA_QK), row(DA_QK), row(DA_WIDTH), row(RW_PROJ)],
        out_shape=[jax.ShapeDtypeStruct((n, DA_QK), F32), jax.ShapeDtypeStruct((n, DA_QK), F32),
                   jax.ShapeDtypeStruct((n, DA_WIDTH), F32), jax.ShapeDtypeStruct((n, RW_PROJ), F32)],
        compiler_params=_cparams(("parallel",)),
        name="in_proj",
    )(x, g, w_bf, cos, sin)


def _rope_tables(pos):
    half = DA_HD // 2
    inv = ROPE_THETA ** (-jnp.arange(half, dtype=F32) / half)
    ang = pos.astype(F32)[:, None] * inv[None, :]
    cos, sin = jnp.cos(ang), jnp.sin(ang)
    cos = jnp.tile(cos, (1, LANES // half))
    sin = jnp.tile(jnp.concatenate([-sin, sin], axis=1), (1, LANES // DA_HD))
    return cos, sin


def _lambda(lq1, lk1, lq2, lk2):
    return (jnp.exp(jnp.sum(lq1 * lk1, axis=-1, keepdims=True))
            - jnp.exp(jnp.sum(lq2 * lk2, axis=-1, keepdims=True)) + LAM_INIT)


def _attn_prompt_kernel(qi_ref, ki_ref, q_ref, k_ref, v_ref, lq1, lk1, lq2, lk2, g_ref, o_ref,
                        m1, l1, a1, m2, l2, a2, *, tq):
    qi = qi_ref[pl.program_id(2)]
    ki = ki_ref[pl.program_id(2)]
    maps = ((m1, l1, a1), (m2, l2, a2))

    @pl.when(ki == 0)
    def _():
        for m, l, a in maps:
            m[...] = jnp.full(m.shape, NEG_INF, F32)
            l[...] = jnp.zeros(l.shape, F32)
            a[...] = jnp.zeros(a.shape, F32)

    def block(diagonal):
        q = q_ref[0] * (DA_HD ** -0.5)
        lane = lax.broadcasted_iota(jnp.int32, q.shape, 1)
        q1 = jnp.where(lane < DA_HD, q, 0.0).astype(BF16)
        q2 = jnp.where(lane < DA_HD, 0.0, q).astype(BF16)
        kb = k_ref[0].astype(BF16)
        vt = v_ref[0].T.astype(BF16)
        dn = (((1,), (1,)), ((), ()))
        state = [(m[...], l[...], a[...]) for m, l, a in maps]
        new = []
        for qm, (m_old, l_old, a_old) in zip((q1, q2), state):
            s = lax.dot_general(kb, qm, dn, preferred_element_type=F32)
            if diagonal:
                key = lax.broadcasted_iota(jnp.int32, (tq, tq), 0)
                qry = lax.broadcasted_iota(jnp.int32, (tq, tq), 1)
                s = jnp.where(key <= qry, s, NEG_INF)
            m_new = jnp.maximum(m_old, jnp.max(s, axis=0, keepdims=True))
            alpha = jnp.exp(m_old - m_new)
            p = jnp.exp(s - m_new)
            new.append((m_new, alpha * l_old + jnp.sum(p, axis=0, keepdims=True),
                        alpha * a_old + jnp.dot(vt, p.astype(BF16), preferred_element_type=F32)))
        for (m, l, a), (m_new, l_new, a_new) in zip(maps, new):
            m[...] = m_new
            l[...] = l_new
            a[...] = a_new

    @pl.when(ki < qi)
    def _():
        block(diagonal=False)

    @pl.when(ki == qi)
    def _():
        block(diagonal=True)
        lam = _lambda(lq1[...], lk1[...], lq2[...], lk2[...])
        o = (a1[...] / l1[...] - lam * (a2[...] / l2[...])).T
        o_ref[0] = _rms(o, g_ref[...]) * (1.0 - LAM_INIT)


def _attn_prompt(q, k, v, lams, subln, tq):
    b, t, _ = q.shape
    nq = t // tq
    pairs = [(i, j) for i in range(nq) for j in range(i + 1)]
    qi_tab = jnp.asarray([i for i, _ in pairs], jnp.int32)
    ki_tab = jnp.asarray([j for _, j in pairs], jnp.int32)
    qspec = pl.BlockSpec((1, tq, DA_VD), lambda bi, h, p, qi, ki: (bi, qi[p], h))
    kspec = pl.BlockSpec((1, tq, DA_VD), lambda bi, h, p, qi, ki: (bi, ki[p], h))
    vec = lambda w: pl.BlockSpec((1, w), lambda bi, h, p, qi, ki: (0, 0))
    grid_spec = pltpu.PrefetchScalarGridSpec(
        num_scalar_prefetch=2,
        grid=(b, DA_HEADS, len(pairs)),
        in_specs=[qspec, kspec, kspec, vec(DA_HD), vec(DA_HD), vec(DA_HD), vec(DA_HD), vec(DA_VD)],
        out_specs=qspec,
        scratch_shapes=[pltpu.VMEM((1, tq), F32), pltpu.VMEM((1, tq), F32), pltpu.VMEM((DA_VD, tq), F32),
                        pltpu.VMEM((1, tq), F32), pltpu.VMEM((1, tq), F32), pltpu.VMEM((DA_VD, tq), F32)],
    )
    return pl.pallas_call(
        functools.partial(_attn_prompt_kernel, tq=tq),
        grid_spec=grid_spec,
        out_shape=jax.ShapeDtypeStruct((b, t, DA_WIDTH), F32),
        compiler_params=_cparams(("parallel", "parallel", "arbitrary")),
        name="attn_prompt",
    )(qi_tab, ki_tab, q, k, v, *lams, subln)


def _attn_sample_kernel(pt_ref, q_ref, *refs, n_steps, pps, t_new):
    ck_refs, cv_refs = refs[:pps], refs[pps:2 * pps]
    kn_ref, vn_ref, lq1, lk1, lq2, lk2, g_ref, o_ref, qb_ref, m_ref, l_ref, acc_ref = refs[2 * pps:]
    p = pl.program_id(1)
    dn = (((1,), (1,)), ((), ()))

    @pl.when(p == 0)
    def _():
        q = q_ref[0] * (DA_HD ** -0.5)
        lane = lax.broadcasted_iota(jnp.int32, (t_new, DA_VD), 1)
        for h in range(DA_HEADS):
            slab = q[:, h * DA_VD:(h + 1) * DA_VD]
            qb_ref[h, 0:t_new, :] = jnp.where(lane < DA_HD, slab, 0.0)
            qb_ref[h, t_new:2 * t_new, :] = jnp.where(lane < DA_HD, 0.0, slab)
        m_ref[...] = jnp.full(m_ref.shape, NEG_INF, F32)
        l_ref[...] = jnp.zeros(l_ref.shape, F32)
        acc_ref[...] = jnp.zeros(acc_ref.shape, F32)

    @pl.when(p < n_steps)
    def _():
        head_rows = [pl.ds(h, PAGE_SIZE, stride=DA_HEADS) for h in range(DA_HEADS)]
        m_old = [m_ref[h] for h in range(DA_HEADS)]
        l_old = [l_ref[h] for h in range(DA_HEADS)]
        acc_old = [acc_ref[h] for h in range(DA_HEADS)]
        qb = [qb_ref[h].astype(BF16) for h in range(DA_HEADS)]
        s = [jnp.concatenate([lax.dot_general(qb[h], ck[head_rows[h], :].astype(BF16), dn, preferred_element_type=F32)
                              for ck in ck_refs], axis=1) for h in range(DA_HEADS)]
        m_new = [jnp.maximum(m_old[h], jnp.max(s[h], axis=-1, keepdims=True)) for h in range(DA_HEADS)]
        alpha = [jnp.exp(m_old[h] - m_new[h]) for h in range(DA_HEADS)]
        pr = [jnp.exp(s[h] - m_new[h]) for h in range(DA_HEADS)]
        l_new = [alpha[h] * l_old[h] + jnp.sum(pr[h], axis=-1, keepdims=True) for h in range(DA_HEADS)]
        acc_new = []
        for h in range(DA_HEADS):
            pv = None
            for i, cv in enumerate(cv_refs):
                term = jnp.dot(pr[h][:, i * PAGE_SIZE:(i + 1) * PAGE_SIZE].astype(BF16),
                               cv[head_rows[h], :].astype(BF16), preferred_element_type=F32)
                pv = term if pv is None else pv + term
            acc_new.append(alpha[h] * acc_old[h] + pv)
        for h in range(DA_HEADS):
            m_ref[h] = m_new[h]
            l_ref[h] = l_new[h]
            acc_ref[h] = acc_new[h]

    @pl.when(p == n_steps)
    def _():
        lam = _lambda(lq1[...], lk1[...], lq2[...], lk2[...])
        t_of_row = lax.broadcasted_iota(jnp.int32, (2 * t_new, 1), 0) % t_new
        for h in range(DA_HEADS):
            qb = qb_ref[h]
            kn = kn_ref[0, :, h * DA_VD:(h + 1) * DA_VD]
            vn = vn_ref[0, :, h * DA_VD:(h + 1) * DA_VD]
            s_new = [jnp.where(t_of_row >= j, jnp.sum(qb * kn[j:j + 1, :], axis=-1, keepdims=True), NEG_INF)
                     for j in range(t_new)]
            m_old = m_ref[h]
            m_new = m_old
            for s_j in s_new:
                m_new = jnp.maximum(m_new, s_j)
            alpha = jnp.exp(m_old - m_new)
            l = alpha * l_ref[h]
            acc = alpha * acc_ref[h]
            for j, s_j in enumerate(s_new):
                p_j = jnp.exp(s_j - m_new)
                l = l + p_j
                acc = acc + p_j * vn[j:j + 1, :]
            o_all = acc / l
            o = o_all[0:t_new] - lam * o_all[t_new:2 * t_new]
            o_ref[0, :, h * DA_VD:(h + 1) * DA_VD] = _rms(o, g_ref[...]) * (1.0 - LAM_INIT)


def _attn_sample(q, k_new, v_new, cache_k, cache_v, page_table, lams, subln):
    bd, t_new, _ = q.shape
    n_pages = page_table.shape[1]
    pps = next(c for c in (8, 4, 2, 1) if n_pages % c == 0)
    n_steps = n_pages // pps
    page_rows = PAGE_SIZE * DA_HEADS
    tok = pl.BlockSpec((1, t_new, DA_WIDTH), lambda b, p, pt: (b, 0, 0))

    def page(i):
        return pl.BlockSpec((page_rows, DA_VD),
                            lambda b, p, pt: (pt[b, jnp.minimum(p, n_steps - 1) * pps + i], 0))

    pages = [page(i) for i in range(pps)]
    vec = lambda w: pl.BlockSpec((1, w), lambda b, p, pt: (0, 0))
    rows = 2 * t_new
    grid_spec = pltpu.PrefetchScalarGridSpec(
        num_scalar_prefetch=1,
        grid=(bd, n_steps + 1),
        in_specs=[tok] + pages + pages + [tok, tok, vec(DA_HD), vec(DA_HD), vec(DA_HD), vec(DA_HD), vec(DA_VD)],
        out_specs=tok,
        scratch_shapes=[pltpu.VMEM((DA_HEADS, rows, DA_VD), F32), pltpu.VMEM((DA_HEADS, rows, 1), F32),
                        pltpu.VMEM((DA_HEADS, rows, 1), F32), pltpu.VMEM((DA_HEADS, rows, DA_VD), F32)],
    )
    return pl.pallas_call(
        functools.partial(_attn_sample_kernel, n_steps=n_steps, pps=pps, t_new=t_new),
        grid_spec=grid_spec,
        out_shape=jax.ShapeDtypeStruct((bd, t_new, DA_WIDTH), F32),
        compiler_params=_cparams(("parallel", "arbitrary")),
        name="attn_sample",
    )(page_table, q, *([cache_k] * pps), *([cache_v] * pps), k_new, v_new, *lams, subln)


def _head_sum(x, seg):
    return jnp.dot(x, seg, preferred_element_type=F32, precision=HIGHEST)


def _rw_pre_kernel(p_ref, first_ref, mu_ref, wcat_ref, w0_ref, a0_ref, kk_ref, ka_ref, rk_ref, seg_ref,
                   r_out, w_out, k_out, v_out, kk_out, b_out, g_out, bonus_out):
    p = p_ref[...]
    gb, rows, _ = p.shape
    row = lax.broadcasted_iota(jnp.int32, p.shape, 1)
    p_prev = jnp.where(row == 0, first_ref[...], pltpu.roll(p, 1, 1))
    ps = (p + (p_prev - p) * mu_ref[...]).reshape(gb * rows, RW_PROJ)
    w_ = RW_WIDTH
    r, k, v = ps[:, :w_], ps[:, w_:2 * w_], ps[:, 2 * w_:3 * w_]
    x = ps[:, 3 * w_:]
    lane = lax.broadcasted_iota(jnp.int32, x.shape, 1)
    act = jnp.where(lane < RW_W_RANK, jnp.tanh(x), jnp.where(lane < RW_W_RANK + RW_A_RANK, x, _sigmoid(x)))
    lora = jnp.dot(act.astype(BF16), wcat_ref[...], preferred_element_type=F32)
    decay = jnp.exp(-math.exp(-0.5) * _sigmoid(w0_ref[...] + lora[:, :w_]))
    a = _sigmoid(a0_ref[...] + lora[:, w_:2 * w_])
    seg = seg_ref[...]
    kk = k * kk_ref[...]
    kk = kk / jnp.maximum(jnp.sqrt(_head_sum(kk * kk, seg)), 1e-12)
    k_mod = k * (1.0 + (a - 1.0) * ka_ref[...])
    r_out[...] = r
    w_out[...] = decay
    k_out[...] = k_mod
    v_out[...] = v
    kk_out[...] = kk
    b_out[...] = kk * a
    g_out[...] = lora[:, 2 * w_:]
    bonus_out[...] = _head_sum(r * k_mod * rk_ref[...], seg) * v


def _rw_pre(p3, first, mu, wcat, w0, a0, k_k, k_a, r_k, seg, gb):
    g, rows, _ = p3.shape
    n = g * rows
    vec = lambda w: _full((1, w))
    out = pl.BlockSpec((gb * rows, RW_WIDTH), lambda i: (i, 0))
    return pl.pallas_call(
        _rw_pre_kernel,
        grid=(g // gb,),
        in_specs=[pl.BlockSpec((gb, rows, RW_PROJ), lambda i: (i, 0, 0)),
                  pl.BlockSpec((gb, 1, RW_PROJ), lambda i: (i, 0, 0)),
                  vec(RW_PROJ), _full((RW_LORA, 3 * RW_WIDTH)), vec(RW_WIDTH), vec(RW_WIDTH),
                  vec(RW_WIDTH), vec(RW_WIDTH), vec(RW_WIDTH), _full((RW_WIDTH, RW_WIDTH))],
        out_specs=[out] * 8,
        out_shape=[jax.ShapeDtypeStruct((n, RW_WIDTH), F32)] * 8,
        compiler_params=_cparams(("parallel",)),
        name="rw_pre",
    )(p3, first, mu, wcat, w0, a0, k_k, k_a, r_k, seg)


def _rw_scan_kernel(r_ref, w_ref, k_ref, v_ref, kk_ref, b_ref, h0_ref, y_ref, h_ref, *, tc):
    @pl.when(pl.program_id(1) == 0)
    def _():
        h_ref[...] = h0_ref[...]

    def key_row(ref, t, j):
        return ref[0, t, pl.ds(j, 1), :]

    def step(t, carry):
        parts = [None] * 4
        for j in range(RW_HD):
            term = h_ref[0, j] * key_row(kk_ref, t, j)
            parts[j % 4] = term if parts[j % 4] is None else parts[j % 4] + term
        sa = -((parts[0] + parts[1]) + (parts[2] + parts[3]))
        v = v_ref[0, t]
        parts = [None] * 4
        for j in range(RW_HD):
            h = h_ref[0, j] * key_row(w_ref, t, j) + sa * key_row(b_ref, t, j) + v * key_row(k_ref, t, j)
            h_ref[0, j] = h
            term = h * key_row(r_ref, t, j)
            parts[j % 4] = term if parts[j % 4] is None else parts[j % 4] + term
        y_ref[0, t] = (parts[0] + parts[1]) + (parts[2] + parts[3])
        return carry

    lax.fori_loop(0, tc, step, 0)


def _rw_scan(r, w, k, v, kk, b, h0, tc):
    nb, t, rows = v.shape[0], v.shape[1], v.shape[2]
    key = pl.BlockSpec((1, tc, RW_HD, LANES), lambda i, c: (i, c, 0, 0))
    val = pl.BlockSpec((1, tc, rows, LANES), lambda i, c: (i, c, 0, 0))
    st = pl.BlockSpec((1, RW_HD, rows, LANES), lambda i, c: (i, 0, 0, 0))
    return pl.pallas_call(
        functools.partial(_rw_scan_kernel, tc=tc),
        grid=(nb, t // tc),
        in_specs=[key, key, key, val, key, key, st],
        out_specs=[val, st],
        out_shape=[jax.ShapeDtypeStruct((nb, t, rows, LANES), F32),
                   jax.ShapeDtypeStruct((nb, RW_HD, rows, LANES), F32)],
        compiler_params=_cparams(("parallel", "arbitrary")),
        name="rw_scan",
    )(r, w, k, v, kk, b, h0)


def _scan_split(n_states):
    slices = max(1, LANES // n_states)
    assert (n_states * slices) % LANES == 0 and RW_HD % slices == 0 and (RW_HD // slices) % SUBLANES == 0, n_states
    return slices, LANES // slices


def _key_cols(x, slices, per_block):
    b, t, h, j = x.shape
    x = x.transpose(1, 3, 0, 2).reshape(t, j, (b * h) // per_block, 1, per_block)
    x = jnp.broadcast_to(x, (t, j, (b * h) // per_block, slices, per_block))
    return x.transpose(2, 0, 1, 3, 4).reshape((b * h) // per_block, t, j, LANES)


def _value_cols(x, slices, per_block):
    b, t, h, i = x.shape
    rows = i // slices
    x = x.reshape(b, t, h, slices, rows).transpose(1, 4, 3, 0, 2).reshape(t, rows, slices, (b * h) // per_block, per_block)
    return x.transpose(3, 0, 1, 2, 4).reshape((b * h) // per_block, t, rows, LANES)


def _value_cols_inv(y, b, h, slices, per_block):
    nb, t, rows, _ = y.shape
    y = y.reshape(nb, t, rows, slices, per_block).transpose(1, 2, 3, 0, 4).reshape(t, rows, slices, b, h)
    return y.transpose(3, 0, 4, 2, 1).reshape(b, t, h, slices * rows)


def _state_cols(s, slices, per_block):
    b, h, i, j = s.shape
    rows = i // slices
    s = s.reshape(b, h, slices, rows, j).transpose(4, 3, 2, 0, 1).reshape(j, rows, slices, (b * h) // per_block, per_block)
    return s.transpose(3, 0, 1, 2, 4).reshape((b * h) // per_block, j, rows, LANES)


def _state_cols_inv(hs, b, h, slices, per_block):
    nb, j, rows, _ = hs.shape
    hs = hs.reshape(nb, j, rows, slices, per_block).transpose(1, 2, 3, 0, 4).reshape(j, rows, slices, b, h)
    return hs.transpose(3, 4, 2, 1, 0).reshape(b, h, slices * rows, j)


def _out_proj_kernel(x_ref, oda_ref, y_ref, bonus_ref, g_ref, lnw_ref, lnb_ref, seg_ref, wo_ref, fn_ref, wq_ref,
                     h_ref, hn_ref, qp_ref):
    seg = seg_ref[...]
    y = y_ref[...]
    d = y - _head_sum(y, seg) * (1.0 / RW_HD)
    var = _head_sum(d * d, seg) * (1.0 / RW_HD)
    yn = d * lax.rsqrt(var + RW_GN_EPS) * lnw_ref[...] + lnb_ref[...]
    o_rw = (yn + bonus_ref[...]) * g_ref[...]
    h = (x_ref[...]
         + jnp.dot(oda_ref[...].astype(BF16), wo_ref[:DA_WIDTH, :], preferred_element_type=F32)
         + jnp.dot(o_rw.astype(BF16), wo_ref[DA_WIDTH:, :], preferred_element_type=F32))
    hn = _rms(h, fn_ref[...])
    h_ref[...] = h
    hn_bf = hn.astype(BF16)
    hn_ref[...] = hn_bf
    qp_ref[...] = jnp.dot(hn_bf, wq_ref[...], preferred_element_type=F32).astype(BF16)


def _out_proj(x, o_da, y, bonus, g, ln_w, ln_b, seg, w_out_bf, ffn_norm, wq_bf, tm):
    n = x.shape[0]
    row = lambda w: pl.BlockSpec((tm, w), lambda i: (i, 0))
    vec = lambda w: _full((1, w))
    return pl.pallas_call(
        _out_proj_kernel,
        grid=(n // tm,),
        in_specs=[row(D_MODEL), row(DA_WIDTH), row(RW_WIDTH), row(RW_WIDTH), row(RW_WIDTH),
                  vec(RW_WIDTH), vec(RW_WIDTH), _full((RW_WIDTH, RW_WIDTH)),
                  _full((DA_WIDTH + RW_WIDTH, D_MODEL)), vec(D_MODEL), _full((D_MODEL, PEER_QW))],
        out_specs=[row(D_MODEL), row(D_MODEL), row(PEER_QW)],
        out_shape=[jax.ShapeDtypeStruct((n, D_MODEL), F32), jax.ShapeDtypeStruct((n, D_MODEL), BF16),
                   jax.ShapeDtypeStruct((n, PEER_QW), BF16)],
        compiler_params=_cparams(("parallel",)),
        name="out_proj",
    )(x, o_da, y, bonus, g, ln_w, ln_b, seg, w_out_bf, ffn_norm, wq_bf)


def _top_k_rows(s, order, payload, k):
    vals, pays = [], []
    for _ in range(k):
        m = jnp.max(s, axis=0, keepdims=True)
        first = jnp.min(jnp.where(s == m, order, 1e9), axis=0, keepdims=True)
        hit = order == first
        vals.append(m)
        pays.append(first if payload is None else jnp.max(jnp.where(hit, payload, -1.0), axis=0, keepdims=True))
        s = jnp.where(hit, -jnp.inf, s)
    return jnp.concatenate(vals, axis=0), jnp.concatenate(pays, axis=0)


_PAIR_BLOCKS = ([("a", a0, 0) for a0 in (0, 8)] + [("a", 0, b) for b in range(1, 8)] + [("b", 0, 8)])


def _pair_candidates(sv, si, tm):
    r8 = lax.broadcasted_iota(jnp.int32, (SUBLANES, tm), 0)
    cand, flat, cidx = [], [], []
    for kind, a0, b0 in _PAIR_BLOCKS:
        if kind == "a":
            a, b = r8 + a0, jnp.full((SUBLANES, tm), b0, jnp.int32)
            val = sv[0][a0:a0 + SUBLANES] + sv[1][b0:b0 + 1]
            idx = si[0][a0:a0 + SUBLANES] * PEER_NKEYS + si[1][b0:b0 + 1]
        else:
            a, b = jnp.full((SUBLANES, tm), a0, jnp.int32), r8 + b0
            val = sv[0][a0:a0 + 1] + sv[1][b0:b0 + SUBLANES]
            idx = si[0][a0:a0 + 1] * PEER_NKEYS + si[1][b0:b0 + SUBLANES]
        ok = (a + 1) * (b + 1) <= PEER_TOPK
        cand.append(jnp.where(ok, val, -jnp.inf))
        flat.append(jnp.where(ok, a * PEER_TOPK + b, 1000000 + a * PEER_TOPK + b).astype(F32))
        cidx.append(idx)
    return jnp.concatenate(cand, axis=0), jnp.concatenate(flat, axis=0), jnp.concatenate(cidx, axis=0)


def _peer_topk_kernel(qp_ref, keys_ref, idx_ref, gate_ref, *, tm):
    key_row = lax.broadcasted_iota(jnp.int32, (PEER_NKEYS, tm), 0).astype(F32)
    dn = (((1,), (1,)), ((), ()))
    idx_rows, gate_rows = [], []
    for h in range(PEER_HEADS):
        sv, si = [], []
        for c in range(2):
            hc = 2 * h + c
            q = qp_ref[:, hc * PEER_KD:(hc + 1) * PEER_KD]
            s = lax.dot_general(keys_ref[hc], q, dn, preferred_element_type=F32)
            v_, i_ = _top_k_rows(s, key_row, None, PEER_TOPK)
            sv.append(v_)
            si.append(i_)
        cand, flat, cidx = _pair_candidates(sv, si, tm)
        bv, eidx = _top_k_rows(cand, flat, cidx, PEER_TOPK)
        e = jnp.exp(bv - bv[0:1, :])
        gate_rows.append(e / jnp.sum(e, axis=0, keepdims=True))
        idx_rows.append(eidx)
    idx_ref[...] = jnp.concatenate(idx_rows, axis=0).T.astype(jnp.int32)
    gate_ref[...] = jnp.concatenate(gate_rows, axis=0).T


def _peer_topk(qp, keys_bf, tm):
    n = qp.shape[0]
    return pl.pallas_call(
        functools.partial(_peer_topk_kernel, tm=tm),
        grid=(n // tm,),
        in_specs=[pl.BlockSpec((tm, PEER_QW), lambda i: (i, 0)),
                  _full((PEER_HEADS * 2, PEER_NKEYS, PEER_KD))],
        out_specs=[pl.BlockSpec((tm, PEER_PICKS), lambda i: (i, 0))] * 2,
        out_shape=[jax.ShapeDtypeStruct((n, PEER_PICKS), jnp.int32),
                   jax.ShapeDtypeStruct((n, PEER_PICKS), F32)],
        compiler_params=_cparams(("parallel",)),
        name="peer_topk",
    )(qp, keys_bf)


GATE_PASSES = 2
GATE_ROWS = PEER_NKEYS // GATE_PASSES
GATE_PITCH = GATE_ROWS + SUBLANES


def _peer_expert_kernel(idx_ref, gate_ref, hn_ref, h_ref, fn_ref, u_ref, v_ref, out_ref, g_ref, acc_ref, *, tm, te):
    c = pl.program_id(1)
    tiles = te // PEER_NKEYS
    chunks_per_pass = GATE_ROWS // tiles
    c_in_pass = lax.rem(c, chunks_per_pass)
    dn = (((1,), (1,)), ((), ()))

    @pl.when(c_in_pass == 0)
    def _():
        first_key = lax.broadcasted_iota(jnp.int32, (GATE_ROWS, PEER_PICKS), 0) + (c // chunks_per_pass) * GATE_ROWS
        second_key = lax.broadcasted_iota(jnp.int32, (PEER_NKEYS, PEER_PICKS), 0)

        def token(t, carry):
            e = idx_ref[pl.ds(t, 1), :]
            g = gate_ref[pl.ds(t, 1), :]
            g_hi = g.astype(BF16).astype(F32)
            hit1 = first_key == lax.shift_right_logical(e, 7)
            hit2 = second_key == (e & (PEER_NKEYS - 1))
            m1 = jnp.where(hit1, 1.0, 0.0).astype(BF16)
            m2_hi = jnp.where(hit2, g_hi, 0.0).astype(BF16)
            m2_lo = jnp.where(hit2, g - g_hi, 0.0).astype(BF16)
            dense = lax.dot_general(jnp.concatenate([m1, m1], axis=1), jnp.concatenate([m2_hi, m2_lo], axis=1), dn,
                                    preferred_element_type=F32)
            g_ref[pl.ds(pl.multiple_of(t * GATE_PITCH, SUBLANES), GATE_ROWS), :] = dense
            return carry

        lax.fori_loop(0, tm, token, 0, unroll=8)

    lin = lax.dot_general(hn_ref[...], u_ref[...], dn, preferred_element_type=F32)
    act = jax.nn.gelu(lin)
    parts = []
    for j in range(tiles):
        gates = g_ref[pl.ds(c_in_pass * tiles + j, tm, stride=GATE_PITCH), :]
        parts.append((act[:, j * PEER_NKEYS:(j + 1) * PEER_NKEYS] * gates).astype(BF16))
    ffn = jnp.dot(jnp.concatenate(parts, axis=1), v_ref[...], preferred_element_type=F32)

    @pl.when(c == 0)
    def _():
        acc_ref[...] = ffn

    @pl.when(c != 0)
    def _():
        acc_ref[...] += ffn

    @pl.when(c == pl.num_programs(1) - 1)
    def _():
        out_ref[...] = _rms(h_ref[...] + acc_ref[...], fn_ref[...])


def _peer_expert(idx, gate, hn, h, final_norm, u_bf, v_bf, tm, te):
    n = hn.shape[0]
    n_exp = u_bf.shape[0]
    row = lambda w: pl.BlockSpec((tm, w), lambda i, c: (i, 0))
    tab = pl.BlockSpec((te, D_MODEL), lambda i, c: (c, 0))
    return pl.pallas_call(
        functools.partial(_peer_expert_kernel, tm=tm, te=te),
        grid=(n // tm, n_exp // te),
        in_specs=[row(PEER_PICKS), row(PEER_PICKS), row(D_MODEL), row(D_MODEL),
                  pl.BlockSpec((1, D_MODEL), lambda i, c: (0, 0)), tab, tab],
        out_specs=row(D_MODEL),
        out_shape=jax.ShapeDtypeStruct((n, D_MODEL), F32),
        scratch_shapes=[pltpu.VMEM((tm * GATE_PITCH, PEER_NKEYS), F32), pltpu.VMEM((tm, D_MODEL), F32)],
        compiler_params=_cparams(("parallel", "arbitrary")),
        name="peer_expert",
    )(idx, gate, hn, h, final_norm, u_bf, v_bf)


def _row_tile(n, target):
    tm = min(n, target)
    while n % tm:
        tm //= 2
    return tm


def _group(x, pos, attend, wkv0, shift0, wts):
    b, t, _ = x.shape
    n = b * t
    x2 = x.reshape(n, D_MODEL)
    tm = _row_tile(n, 256)
    cos, sin = _rope_tables(jnp.tile(pos, b))
    q, k, v, p_rw = _in_proj(x2, wts["attn_norm"], wts["w_in"], cos, sin, tm)

    o_da = attend(q.reshape(b, t, DA_QK), k.reshape(b, t, DA_QK), v.reshape(b, t, DA_WIDTH))

    rows = _row_tile(t, 256)
    p3 = p_rw.reshape(n // rows, rows, RW_PROJ)
    p_bt = p_rw.reshape(b, t, RW_PROJ)
    prev_rows = p_bt[:, rows - 1::rows][:, :t // rows - 1]
    first = jnp.concatenate([shift0[:, None], prev_rows], axis=1).reshape(n // rows, 1, RW_PROJ)
    gb = _row_tile(n // rows, max(1, 256 // rows))
    r, w, k_mod, v_rw, kk, bb_, g, bonus = _rw_pre(
        p3, first, wts["rw_mu"], wts["rw_wcat"], wts["rw_w0"], wts["rw_a0"],
        wts["rw_k_k"], wts["rw_k_a"], wts["rw_r_k"], wts["seg"], gb)
    slices, per_block = _scan_split(b * RW_HEADS)
    heads = lambda a: a.reshape(b, t, RW_HEADS, RW_HD)
    key = lambda a: _key_cols(heads(a), slices, per_block)
    y_cols, h_cols = _rw_scan(key(r), key(w), key(k_mod), _value_cols(heads(v_rw), slices, per_block),
                              key(kk), key(bb_), _state_cols(wkv0, slices, per_block), tc=_row_tile(t, 64))
    y = _value_cols_inv(y_cols, b, RW_HEADS, slices, per_block)
    wkv = _state_cols_inv(h_cols, b, RW_HEADS, slices, per_block)

    h, hn, qp = _out_proj(x2, o_da.reshape(n, DA_WIDTH), y.reshape(n, RW_WIDTH), bonus, g,
                          wts["rw_ln_w"], wts["rw_ln_b"], wts["seg"], wts["w_out"], wts["ffn_norm"],
                          wts["peer_wq"], tm)
    idx, gate = _peer_topk(qp, wts["peer_keys"], tm)
    out = _peer_expert(idx, gate, hn, h, wts["final_norm"], wts["peer_u"], wts["peer_v"], _row_tile(n, 512), te=1024)
    return (out.reshape(b, t, D_MODEL), k.reshape(b, t, DA_HEADS, 2 * DA_HD), v.reshape(b, t, DA_HEADS, DA_VD),
            wkv, p_bt[:, -1])


def kernel(x_prompt, x_sample, cache_k, cache_v, state_wkv, state_shift, page_table, attn_norm, w_in, w_out,
           da_lambda_q1, da_lambda_k1, da_lambda_q2, da_lambda_k2, da_subln, rw_mu, rw_w0, rw_w_up, rw_a0,
           rw_a_up, rw_g_up, rw_k_k, rw_k_a, rw_r_k, rw_ln_w, rw_ln_b, ffn_norm, peer_wq, peer_sub_keys,
           peer_u, peer_v, final_norm):
    assert w_in.shape[0] == 1, "single-layer trunk"
    b, t_p, _ = x_prompt.shape
    bd, t_s, _ = x_sample.shape
    past = page_table.shape[1] * PAGE_SIZE
    vec = lambda a: a.reshape(1, -1).astype(F32)

    wcat = jnp.zeros((RW_LORA, 3 * RW_WIDTH), F32)
    wcat = wcat.at[:RW_W_RANK, :RW_WIDTH].set(rw_w_up[0])
    wcat = wcat.at[RW_W_RANK:RW_W_RANK + RW_A_RANK, RW_WIDTH:2 * RW_WIDTH].set(rw_a_up[0])
    wcat = wcat.at[RW_W_RANK + RW_A_RANK:, 2 * RW_WIDTH:].set(rw_g_up[0])
    lane_head = jnp.arange(RW_WIDTH) // RW_HD
    wts = dict(
        attn_norm=vec(attn_norm[0]), w_in=w_in[0].astype(BF16), w_out=w_out[0].astype(BF16),
        rw_mu=vec(rw_mu[0]), rw_wcat=wcat.astype(BF16), rw_w0=vec(rw_w0[0]), rw_a0=vec(rw_a0[0]),
        rw_k_k=vec(rw_k_k[0]), rw_k_a=vec(rw_k_a[0]), rw_r_k=vec(rw_r_k[0]),
        rw_ln_w=vec(rw_ln_w[0]), rw_ln_b=vec(rw_ln_b[0]),
        seg=(lane_head[:, None] == lane_head[None, :]).astype(F32),
        ffn_norm=vec(ffn_norm[0]), peer_wq=peer_wq[0].astype(BF16),
        peer_keys=peer_sub_keys[0].reshape(PEER_HEADS * 2, PEER_NKEYS, PEER_KD).astype(BF16),
        peer_u=peer_u.reshape(-1, D_MODEL).astype(BF16), peer_v=peer_v.reshape(-1, D_MODEL).astype(BF16),
        final_norm=vec(final_norm),
    )
    lams = (vec(da_lambda_q1[0]), vec(da_lambda_k1[0]), vec(da_lambda_q2[0]), vec(da_lambda_k2[0]))
    subln = vec(da_subln[0])

    att_p = lambda q, k, v: _attn_prompt(q, k, v, lams, subln, _row_tile(t_p, 512))
    ck = cache_k.reshape(-1, DA_VD)
    cv = cache_v.reshape(-1, DA_VD)
    att_s = lambda q, k, v: _attn_sample(q, k, v, ck, cv, page_table, lams, subln)

    pos_p = jnp.arange(t_p, dtype=jnp.int32)
    pos_s = past + jnp.arange(t_s, dtype=jnp.int32)
    yp, kp, vp, wp, sp = _group(x_prompt, pos_p, att_p, jnp.zeros((b, RW_HEADS, RW_HD, RW_HD), F32),
                                jnp.zeros((b, RW_PROJ), F32), wts)
    ys, ks, vs, ws, ss = _group(x_sample, pos_s, att_s, state_wkv[0], state_shift[0], wts)
    return (yp, ys, kp[None], vp[None], wp[None], sp[None], ks[None], vs[None], ws[None], ss[None])
```

```python
import functools
import math

import jax
import jax.numpy as jnp
from jax import lax
from jax.experimental import pallas as pl
from jax.experimental.pallas import tpu as pltpu

F32 = jnp.float32
BF16 = jnp.bfloat16

D_MODEL = 1024
PAGE_SIZE = 128
DA_HEADS = 4
DA_HD = 64
DA_VD = 2 * DA_HD
DA_WIDTH = DA_HEADS * DA_VD
DA_QK = DA_HEADS * 2 * DA_HD
RW_HEADS = 8
RW_HD = 64
RW_WIDTH = RW_HEADS * RW_HD
RW_W_RANK = 64
RW_A_RANK = 64
RW_G_RANK = 128
RW_LORA = RW_W_RANK + RW_A_RANK + RW_G_RANK
RW_PROJ = 3 * RW_WIDTH + RW_LORA
RW_GN_EPS = 64e-5
IN_PROJ = 2 * DA_QK + DA_WIDTH + RW_PROJ
PEER_HEADS = 8
PEER_NKEYS = 128
PEER_KD = 128
PEER_TOPK = 16
PEER_PICKS = PEER_HEADS * PEER_TOPK
PEER_QW = PEER_HEADS * 2 * PEER_KD
ROPE_THETA = 10000.0
NORM_EPS = 1e-6
NEG_INF = -1e30
LAM_INIT = 0.8 - 0.6 * math.exp(-0.3 * 0)

LANES = 128
SUBLANES = 8
VMEM_LIMIT_BYTES = 56 * 1024 * 1024

HIGHEST = lax.Precision.HIGHEST


def _cparams(sem):
    return pltpu.CompilerParams(dimension_semantics=sem, vmem_limit_bytes=VMEM_LIMIT_BYTES)


def _full(shape):
    return pl.BlockSpec(shape, lambda *_: (0,) * len(shape))


def _rms(x, g):
    return x * lax.rsqrt(jnp.mean(x * x, axis=-1, keepdims=True) + NORM_EPS) * g


def _sigmoid(x):
    return 1.0 / (1.0 + jnp.exp(-x))


def _rope_slab(t, cos, sin_signed):
    lane = lax.broadcasted_iota(jnp.int32, t.shape, 1)
    swapped = jnp.where(lane % DA_HD < DA_HD // 2,
                        pltpu.roll(t, LANES - DA_HD // 2, 1),
                        pltpu.roll(t, DA_HD // 2, 1))
    return t * cos + swapped * sin_signed


def _in_proj_kernel(x_ref, g_ref, w_ref, cos_ref, sin_ref, q_ref, k_ref, v_ref, p_ref):
    xn = _rms(x_ref[...], g_ref[...])
    proj = jnp.dot(xn.astype(BF16), w_ref[...], preferred_element_type=F32)
    cos = cos_ref[...]
    sin = sin_ref[...]
    for s in range(DA_QK // LANES):
        lo = s * LANES
        q_ref[:, lo:lo + LANES] = _rope_slab(proj[:, lo:lo + LANES], cos, sin)
        k_ref[:, lo:lo + LANES] = _rope_slab(proj[:, DA_QK + lo:DA_QK + lo + LANES], cos, sin)
    v_ref[...] = proj[:, 2 * DA_QK:2 * DA_QK + DA_WIDTH]
    p_ref[...] = proj[:, 2 * DA_QK + DA_WIDTH:]


def _in_proj(x, g, w_bf, cos, sin, tm):
    n = x.shape[0]
    row = lambda w: pl.BlockSpec((tm, w), lambda i: (i, 0))
    return pl.pallas_call(
        _in_proj_kernel,
        grid=(n // tm,),
        in_specs=[row(D_MODEL), _full((1, D_MODEL)), _full((D_MODEL, IN_PROJ)), row(LANES), row(LANES)],
        out_specs=[row(DA_QK), row(DA_QK), row(DA_WIDTH), row(RW_PROJ)],
        out_shape=[jax.ShapeDtypeStruct((n, DA_QK), F32), jax.ShapeDtypeStruct((n, DA_QK), F32),
                   jax.ShapeDtypeStruct((n, DA_WIDTH), F32), jax.ShapeDtypeStruct((n, RW_PROJ), F32)],
        compiler_params=_cparams(("parallel",)),
        name="in_proj",
    )(x, g, w_bf, cos, sin)


def _rope_tables(pos):
    half = DA_HD // 2
    inv = ROPE_THETA ** (-jnp.arange(half, dtype=F32) / half)
    ang = pos.astype(F32)[:, None] * inv[None, :]
    cos, sin = jnp.cos(ang), jnp.sin(ang)
    cos = jnp.tile(cos, (1, LANES // half))
    sin = jnp.tile(jnp.concatenate([-sin, sin], axis=1), (1, LANES // DA_HD))
    return cos, sin


def _lambda(lq1, lk1, lq2, lk2):
    return (jnp.exp(jnp.sum(lq1 * lk1, axis=-1, keepdims=True))
            - jnp.exp(jnp.sum(lq2 * lk2, axis=-1, keepdims=True)) + LAM_INIT)


def _attn_prompt_kernel(qi_ref, ki_ref, q_ref, k_ref, v_ref, lq1, lk1, lq2, lk2, g_ref, o_ref,
                        m1, l1, a1, m2, l2, a2, *, tq):
    qi = qi_ref[pl.program_id(2)]
    ki = ki_ref[pl.program_id(2)]
    maps = ((m1, l1, a1), (m2, l2, a2))

    @pl.when(ki == 0)
    def _():
        for m, l, a in maps:
            m[...] = jnp.full(m.shape, NEG_INF, F32)
            l[...] = jnp.zeros(l.shape, F32)
            a[...] = jnp.zeros(a.shape, F32)

    def block(diagonal):
        q = q_ref[0] * (DA_HD ** -0.5)
        lane = lax.broadcasted_iota(jnp.int32, q.shape, 1)
        q1 = jnp.where(lane < DA_HD, q, 0.0).astype(BF16)
        q2 = jnp.where(lane < DA_HD, 0.0, q).astype(BF16)
        kb = k_ref[0].astype(BF16)
        vt = v_ref[0].T.astype(BF16)
        dn = (((1,), (1,)), ((), ()))
        state = [(m[...], l[...], a[...]) for m, l, a in maps]
        new = []
        for qm, (m_old, l_old, a_old) in zip((q1, q2), state):
            s = lax.dot_general(kb, qm, dn, preferred_element_type=F32)
            if diagonal:
                key = lax.broadcasted_iota(jnp.int32, (tq, tq), 0)
                qry = lax.broadcasted_iota(jnp.int32, (tq, tq), 1)
                s = jnp.where(key <= qry, s, NEG_INF)
            m_new = jnp.maximum(m_old, jnp.max(s, axis=0, keepdims=True))
            alpha = jnp.exp(m_old - m_new)
            p = jnp.exp(s - m_new)
            new.append((m_new, alpha * l_old + jnp.sum(p, axis=0, keepdims=True),
                        alpha * a_old + jnp.dot(vt, p.astype(BF16), preferred_element_type=F32)))
        for (m, l, a), (m_new, l_new, a_new) in zip(maps, new):
            m[...] = m_new
            l[...] = l_new
            a[...] = a_new

    @pl.when(ki < qi)
    def _():
        block(diagonal=False)

    @pl.when(ki == qi)
    def _():
        block(diagonal=True)
        lam = _lambda(lq1[...], lk1[...], lq2[...], lk2[...])
        o = (a1[...] / l1[...] - lam * (a2[...] / l2[...])).T
        o_ref[0] = _rms(o, g_ref[...]) * (1.0 - LAM_INIT)


def _attn_prompt(q, k, v, lams, subln, tq):
    b, t, _ = q.shape
    nq = t // tq
    pairs = [(i, j) for i in range(nq) for j in range(i + 1)]
    qi_tab = jnp.asarray([i for i, _ in pairs], jnp.int32)
    ki_tab = jnp.asarray([j for _, j in pairs], jnp.int32)
    qspec = pl.BlockSpec((1, tq, DA_VD), lambda bi, h, p, qi, ki: (bi, qi[p], h))
    kspec = pl.BlockSpec((1, tq, DA_VD), lambda bi, h, p, qi, ki: (bi, ki[p], h))
    vec = lambda w: pl.BlockSpec((1, w), lambda bi, h, p, qi, ki: (0, 0))
    grid_spec = pltpu.PrefetchScalarGridSpec(
        num_scalar_prefetch=2,
        grid=(b, DA_HEADS, len(pairs)),
        in_specs=[qspec, kspec, kspec, vec(DA_HD), vec(DA_HD), vec(DA_HD), vec(DA_HD), vec(DA_VD)],
        out_specs=qspec,
        scratch_shapes=[pltpu.VMEM((1, tq), F32), pltpu.VMEM((1, tq), F32), pltpu.VMEM((DA_VD, tq), F32),
                        pltpu.VMEM((1, tq), F32), pltpu.VMEM((1, tq), F32), pltpu.VMEM((DA_VD, tq), F32)],
    )
    return pl.pallas_call(
        functools.partial(_attn_prompt_kernel, tq=tq),
        grid_spec=grid_spec,
        out_shape=jax.ShapeDtypeStruct((b, t, DA_WIDTH), F32),
        compiler_params=_cparams(("parallel", "parallel", "arbitrary")),
        name="attn_prompt",
    )(qi_tab, ki_tab, q, k, v, *lams, subln)


def _attn_sample_kernel(pt_ref, q_ref, *refs, n_steps, pps, t_new):
    ck_refs, cv_refs = refs[:pps], refs[pps:2 * pps]
    kn_ref, vn_ref, lq1, lk1, lq2, lk2, g_ref, o_ref, qb_ref, m_ref, l_ref, acc_ref = refs[2 * pps:]
    p = pl.program_id(1)
    dn = (((1,), (1,)), ((), ()))

    @pl.when(p == 0)
    def _():
        q = q_ref[0] * (DA_HD ** -0.5)
        lane = lax.broadcasted_iota(jnp.int32, (t_new, DA_VD), 1)
        for h in range(DA_HEADS):
            slab = q[:, h * DA_VD:(h + 1) * DA_VD]
            qb_ref[h, 0:t_new, :] = jnp.where(lane < DA_HD, slab, 0.0)
            qb_ref[h, t_new:2 * t_new, :] = jnp.where(lane < DA_HD, 0.0, slab)
        m_ref[...] = jnp.full(m_ref.shape, NEG_INF, F32)
        l_ref[...] = jnp.zeros(l_ref.shape, F32)
        acc_ref[...] = jnp.zeros(acc_ref.shape, F32)

    @pl.when(p < n_steps)
    def _():
        head_rows = [pl.ds(h, PAGE_SIZE, stride=DA_HEADS) for h in range(DA_HEADS)]
        m_old = [m_ref[h] for h in range(DA_HEADS)]
        l_old = [l_ref[h] for h in range(DA_HEADS)]
        acc_old = [acc_ref[h] for h in range(DA_HEADS)]
        qb = [qb_ref[h].astype(BF16) for h in range(DA_HEADS)]
        s = [jnp.concatenate([lax.dot_general(qb[h], ck[head_rows[h], :].astype(BF16), dn, preferred_element_type=F32)
                              for ck in ck_refs], axis=1) for h in range(DA_HEADS)]
        m_new = [jnp.maximum(m_old[h], jnp.max(s[h], axis=-1, keepdims=True)) for h in range(DA_HEADS)]
        alpha = [jnp.exp(m_old[h] - m_new[h]) for h in range(DA_HEADS)]
        pr = [jnp.exp(s[h] - m_new[h]) for h in range(DA_HEADS)]
        l_new = [alpha[h] * l_old[h] + jnp.sum(pr[h], axis=-1, keepdims=True) for h in range(DA_HEADS)]
        acc_new = []
        for h in range(DA_HEADS):
            pv = None
            for i, cv in enumerate(cv_refs):
                term = jnp.dot(pr[h][:, i * PAGE_SIZE:(i + 1) * PAGE_SIZE].astype(BF16),
                               cv[head_rows[h], :].astype(BF16), preferred_element_type=F32)
                pv = term if pv is None else pv + term
            acc_new.append(alpha[h] * acc_old[h] + pv)
        for h in range(DA_HEADS):
            m_ref[h] = m_new[h]
            l_ref[h] = l_new[h]
            acc_ref[h] = acc_new[h]

    @pl.when(p == n_steps)
    def _():
        lam = _lambda(lq1[...], lk1[...], lq2[...], lk2[...])
        t_of_row = lax.broadcasted_iota(jnp.int32, (2 * t_new, 1), 0) % t_new
        for h in range(DA_HEADS):
            qb = qb_ref[h]
            kn = kn_ref[0, :, h * DA_VD:(h + 1) * DA_VD]
            vn = vn_ref[0, :, h * DA_VD:(h + 1) * DA_VD]
            s_new = [jnp.where(t_of_row >= j, jnp.sum(qb * kn[j:j + 1, :], axis=-1, keepdims=True), NEG_INF)
                     for j in range(t_new)]
            m_old = m_ref[h]
            m_new = m_old
            for s_j in s_new:
                m_new = jnp.maximum(m_new, s_j)
            alpha = jnp.exp(m_old - m_new)
            l = alpha * l_ref[h]
            acc = alpha * acc_ref[h]
            for j, s_j in enumerate(s_new):
                p_j = jnp.exp(s_j - m_new)
                l = l + p_j
                acc = acc + p_j * vn[j:j + 1, :]
            o_all = acc / l
            o = o_all[0:t_new] - lam * o_all[t_new:2 * t_new]
            o_ref[0, :, h * DA_VD:(h + 1) * DA_VD] = _rms(o, g_ref[...]) * (1.0 - LAM_INIT)


def _attn_sample(q, k_new, v_new, cache_k, cache_v, page_table, lams, subln):
    bd, t_new, _ = q.shape
    n_pages = page_table.shape[1]
    pps = next(c for c in (8, 4, 2, 1) if n_pages % c == 0)
    n_steps = n_pages // pps
    page_rows = PAGE_SIZE * DA_HEADS
    tok = pl.BlockSpec((1, t_new, DA_WIDTH), lambda b, p, pt: (b, 0, 0))

    def page(i):
        return pl.BlockSpec((page_rows, DA_VD),
                            lambda b, p, pt: (pt[b, jnp.minimum(p, n_steps - 1) * pps + i], 0))

    pages = [page(i) for i in range(pps)]
    vec = lambda w: pl.BlockSpec((1, w), lambda b, p, pt: (0, 0))
    rows = 2 * t_new
    grid_spec = pltpu.PrefetchScalarGridSpec(
        num_scalar_prefetch=1,
        grid=(bd, n_steps + 1),
        in_specs=[tok] + pages + pages + [tok, tok, vec(DA_HD), vec(DA_HD), vec(DA_HD), vec(DA_HD), vec(DA_VD)],
        out_specs=tok,
        scratch_shapes=[pltpu.VMEM((DA_HEADS, rows, DA_VD), F32), pltpu.VMEM((DA_HEADS, rows, 1), F32),
                        pltpu.VMEM((DA_HEADS, rows, 1), F32), pltpu.VMEM((DA_HEADS, rows, DA_VD), F32)],
    )
    return pl.pallas_call(
        functools.partial(_attn_sample_kernel, n_steps=n_steps, pps=pps, t_new=t_new),
        grid_spec=grid_spec,
        out_shape=jax.ShapeDtypeStruct((bd, t_new, DA_WIDTH), F32),
        compiler_params=_cparams(("parallel", "arbitrary")),
        name="attn_sample",
    )(page_table, q, *([cache_k] * pps), *([cache_v] * pps), k_new, v_new, *lams, subln)


def _head_sum(x, seg):
    return jnp.dot(x, seg, preferred_element_type=F32, precision=HIGHEST)


def _rw_pre_kernel(p_ref, first_ref, mu_ref, wcat_ref, w0_ref, a0_ref, kk_ref, ka_ref, rk_ref, seg_ref,
                   r_out, w_out, k_out, v_out, kk_out, b_out, g_out, bonus_out):
    p = p_ref[...]
    gb, rows, _ = p.shape
    row = lax.broadcasted_iota(jnp.int32, p.shape, 1)
    p_prev = jnp.where(row == 0, first_ref[...], pltpu.roll(p, 1, 1))
    ps = (p + (p_prev - p) * mu_ref[...]).reshape(gb * rows, RW_PROJ)
    w_ = RW_WIDTH
    r, k, v = ps[:, :w_], ps[:, w_:2 * w_], ps[:, 2 * w_:3 * w_]
    x = ps[:, 3 * w_:]
    lane = lax.broadcasted_iota(jnp.int32, x.shape, 1)
    act = jnp.where(lane < RW_W_RANK, jnp.tanh(x), jnp.where(lane < RW_W_RANK + RW_A_RANK, x, _sigmoid(x)))
    lora = jnp.dot(act.astype(BF16), wcat_ref[...], preferred_element_type=F32)
    decay = jnp.exp(-math.exp(-0.5) * _sigmoid(w0_ref[...] + lora[:, :w_]))
    a = _sigmoid(a0_ref[...] + lora[:, w_:2 * w_])
    seg = seg_ref[...]
    kk = k * kk_ref[...]
    kk = kk / jnp.maximum(jnp.sqrt(_head_sum(kk * kk, seg)), 1e-12)
    k_mod = k * (1.0 + (a - 1.0) * ka_ref[...])
    r_out[...] = r
    w_out[...] = decay
    k_out[...] = k_mod
    v_out[...] = v
    kk_out[...] = kk
    b_out[...] = kk * a
    g_out[...] = lora[:, 2 * w_:]
    bonus_out[...] = _head_sum(r * k_mod * rk_ref[...], seg) * v


def _rw_pre(p3, first, mu, wcat, w0, a0, k_k, k_a, r_k, seg, gb):
    g, rows, _ = p3.shape
    n = g * rows
    vec = lambda w: _full((1, w))
    out = pl.BlockSpec((gb * rows, RW_WIDTH), lambda i: (i, 0))
    return pl.pallas_call(
        _rw_pre_kernel,
        grid=(g // gb,),
        in_specs=[pl.BlockSpec((gb, rows, RW_PROJ), lambda i: (i, 0, 0)),
                  pl.BlockSpec((gb, 1, RW_PROJ), lambda i: (i, 0, 0)),
                  vec(RW_PROJ), _full((RW_LORA, 3 * RW_WIDTH)), vec(RW_WIDTH), vec(RW_WIDTH),
                  vec(RW_WIDTH), vec(RW_WIDTH), vec(RW_WIDTH), _full((RW_WIDTH, RW_WIDTH))],
        out_specs=[out] * 8,
        out_shape=[jax.ShapeDtypeStruct((n, RW_WIDTH), F32)] * 8,
        compiler_params=_cparams(("parallel",)),
        name="rw_pre",
    )(p3, first, mu, wcat, w0, a0, k_k, k_a, r_k, seg)


def _rw_scan_kernel(r_ref, w_ref, k_ref, v_ref, kk_ref, b_ref, h0_ref, y_ref, h_ref, *, tc):
    @pl.when(pl.program_id(1) == 0)
    def _():
        h_ref[...] = h0_ref[...]

    def key_row(ref, t, j):
        return ref[0, t, pl.ds(j, 1), :]

    def step(t, carry):
        parts = [None] * 4
        for j in range(RW_HD):
            term = h_ref[0, j] * key_row(kk_ref, t, j)
            parts[j % 4] = term if parts[j % 4] is None else parts[j % 4] + term
        sa = -((parts[0] + parts[1]) + (parts[2] + parts[3]))
        v = v_ref[0, t]
        parts = [None] * 4
        for j in range(RW_HD):
            h = h_ref[0, j] * key_row(w_ref, t, j) + sa * key_row(b_ref, t, j) + v * key_row(k_ref, t, j)
            h_ref[0, j] = h
            term = h * key_row(r_ref, t, j)
            parts[j % 4] = term if parts[j % 4] is None else parts[j % 4] + term
        y_ref[0, t] = (parts[0] + parts[1]) + (parts[2] + parts[3])
        return carry

    lax.fori_loop(0, tc, step, 0)


def _rw_scan(r, w, k, v, kk, b, h0, tc):
    nb, t, rows = v.shape[0], v.shape[1], v.shape[2]
    key = pl.BlockSpec((1, tc, RW_HD, LANES), lambda i, c: (i, c, 0, 0))
    val = pl.BlockSpec((1, tc, rows, LANES), lambda i, c: (i, c, 0, 0))
    st = pl.BlockSpec((1, RW_HD, rows, LANES), lambda i, c: (i, 0, 0, 0))
    return pl.pallas_call(
        functools.partial(_rw_scan_kernel, tc=tc),
        grid=(nb, t // tc),
        in_specs=[key, key, key, val, key, key, st],
        out_specs=[val, st],
        out_shape=[jax.ShapeDtypeStruct((nb, t, rows, LANES), F32),
                   jax.ShapeDtypeStruct((nb, RW_HD, rows, LANES), F32)],
        compiler_params=_cparams(("parallel", "arbitrary")),
        name="rw_scan",
    )(r, w, k, v, kk, b, h0)


def _scan_split(n_states):
    slices = max(1, LANES // n_states)
    assert (n_states * slices) % LANES == 0 and RW_HD % slices == 0 and (RW_HD // slices) % SUBLANES == 0, n_states
    return slices, LANES // slices


def _key_cols(x, slices, per_block):
    b, t, h, j = x.shape
    x = x.transpose(1, 3, 0, 2).reshape(t, j, (b * h) // per_block, 1, per_block)
    x = jnp.broadcast_to(x, (t, j, (b * h) // per_block, slices, per_block))
    return x.transpose(2, 0, 1, 3, 4).reshape((b * h) // per_block, t, j, LANES)


def _value_cols(x, slices, per_block):
    b, t, h, i = x.shape
    rows = i // slices
    x = x.reshape(b, t, h, slices, rows).transpose(1, 4, 3, 0, 2).reshape(t, rows, slices, (b * h) // per_block, per_block)
    return x.transpose(3, 0, 1, 2, 4).reshape((b * h) // per_block, t, rows, LANES)


def _value_cols_inv(y, b, h, slices, per_block):
    nb, t, rows, _ = y.shape
    y = y.reshape(nb, t, rows, slices, per_block).transpose(1, 2, 3, 0, 4).reshape(t, rows, slices, b, h)
    return y.transpose(3, 0, 4, 2, 1).reshape(b, t, h, slices * rows)


def _state_cols(s, slices, per_block):
    b, h, i, j = s.shape
    rows = i // slices
    s = s.reshape(b, h, slices, rows, j).transpose(4, 3, 2, 0, 1).reshape(j, rows, slices, (b * h) // per_block, per_block)
    return s.transpose(3, 0, 1, 2, 4).reshape((b * h) // per_block, j, rows, LANES)


def _state_cols_inv(hs, b, h, slices, per_block):
    nb, j, rows, _ = hs.shape
    hs = hs.reshape(nb, j, rows, slices, per_block).transpose(1, 2, 3, 0, 4).reshape(j, rows, slices, b, h)
    return hs.transpose(3, 4, 2, 1, 0).reshape(b, h, slices * rows, j)


def _out_proj_kernel(x_ref, oda_ref, y_ref, bonus_ref, g_ref, lnw_ref, lnb_ref, seg_ref, wo_ref, fn_ref, wq_ref,
                     h_ref, hn_ref, qp_ref):
    seg = seg_ref[...]
    y = y_ref[...]
    d = y - _head_sum(y, seg) * (1.0 / RW_HD)
    var = _head_sum(d * d, seg) * (1.0 / RW_HD)
    yn = d * lax.rsqrt(var + RW_GN_EPS) * lnw_ref[...] + lnb_ref[...]
    o_rw = (yn + bonus_ref[...]) * g_ref[...]
    h = (x_ref[...]
         + jnp.dot(oda_ref[...].astype(BF16), wo_ref[:DA_WIDTH, :], preferred_element_type=F32)
         + jnp.dot(o_rw.astype(BF16), wo_ref[DA_WIDTH:, :], preferred_element_type=F32))
    hn = _rms(h, fn_ref[...])
    h_ref[...] = h
    hn_bf = hn.astype(BF16)
    hn_ref[...] = hn_bf
    qp_ref[...] = jnp.dot(hn_bf, wq_ref[...], preferred_element_type=F32).astype(BF16)


def _out_proj(x, o_da, y, bonus, g, ln_w, ln_b, seg, w_out_bf, ffn_norm, wq_bf, tm):
    n = x.shape[0]
    row = lambda w: pl.BlockSpec((tm, w), lambda i: (i, 0))
    vec = lambda w: _full((1, w))
    return pl.pallas_call(
        _out_proj_kernel,
        grid=(n // tm,),
        in_specs=[row(D_MODEL), row(DA_WIDTH), row(RW_WIDTH), row(RW_WIDTH), row(RW_WIDTH),
                  vec(RW_WIDTH), vec(RW_WIDTH), _full((RW_WIDTH, RW_WIDTH)),
                  _full((DA_WIDTH + RW_WIDTH, D_MODEL)), vec(D_MODEL), _full((D_MODEL, PEER_QW))],
        out_specs=[row(D_MODEL), row(D_MODEL), row(PEER_QW)],
        out_shape=[jax.ShapeDtypeStruct((n, D_MODEL), F32), jax.ShapeDtypeStruct((n, D_MODEL), BF16),
                   jax.ShapeDtypeStruct((n, PEER_QW), BF16)],
        compiler_params=_cparams(("parallel",)),
        name="out_proj",
    )(x, o_da, y, bonus, g, ln_w, ln_b, seg, w_out_bf, ffn_norm, wq_bf)


def _top_k_rows(s, order, payload, k):
    vals, pays = [], []
    for _ in range(k):
        m = jnp.max(s, axis=0, keepdims=True)
        first = jnp.min(jnp.where(s == m, order, 1e9), axis=0, keepdims=True)
        hit = order == first
        vals.append(m)
        pays.append(first if payload is None else jnp.max(jnp.where(hit, payload, -1.0), axis=0, keepdims=True))
        s = jnp.where(hit, -jnp.inf, s)
    return jnp.concatenate(vals, axis=0), jnp.concatenate(pays, axis=0)


_PAIR_BLOCKS = ([("a", a0, 0) for a0 in (0, 8)] + [("a", 0, b) for b in range(1, 8)] + [("b", 0, 8)])


def _pair_candidates(sv, si, tm):
    r8 = lax.broadcasted_iota(jnp.int32, (SUBLANES, tm), 0)
    cand, flat, cidx = [], [], []
    for kind, a0, b0 in _PAIR_BLOCKS:
        if kind == "a":
            a, b = r8 + a0, jnp.full((SUBLANES, tm), b0, jnp.int32)
            val = sv[0][a0:a0 + SUBLANES] + sv[1][b0:b0 + 1]
            idx = si[0][a0:a0 + SUBLANES] * PEER_NKEYS + si[1][b0:b0 + 1]
        else:
            a, b = jnp.full((SUBLANES, tm), a0, jnp.int32), r8 + b0
            val = sv[0][a0:a0 + 1] + sv[1][b0:b0 + SUBLANES]
            idx = si[0][a0:a0 + 1] * PEER_NKEYS + si[1][b0:b0 + SUBLANES]
        ok = (a + 1) * (b + 1) <= PEER_TOPK
        cand.append(jnp.where(ok, val, -jnp.inf))
        flat.append(jnp.where(ok, a * PEER_TOPK + b, 1000000 + a * PEER_TOPK + b).astype(F32))
        cidx.append(idx)
    return jnp.concatenate(cand, axis=0), jnp.concatenate(flat, axis=0), jnp.concatenate(cidx, axis=0)


def _peer_topk_kernel(qp_ref, keys_ref, idx_ref, gate_ref, *, tm):
    key_row = lax.broadcasted_iota(jnp.int32, (PEER_NKEYS, tm), 0).astype(F32)
    dn = (((1,), (1,)), ((), ()))
    idx_rows, gate_rows = [], []
    for h in range(PEER_HEADS):
        sv, si = [], []
        for c in range(2):
            hc = 2 * h + c
            q = qp_ref[:, hc * PEER_KD:(hc + 1) * PEER_KD]
            s = lax.dot_general(keys_ref[hc], q, dn, preferred_element_type=F32)
            v_, i_ = _top_k_rows(s, key_row, None, PEER_TOPK)
            sv.append(v_)
            si.append(i_)
        cand, flat, cidx = _pair_candidates(sv, si, tm)
        bv, eidx = _top_k_rows(cand, flat, cidx, PEER_TOPK)
        e = jnp.exp(bv - bv[0:1, :])
        gate_rows.append(e / jnp.sum(e, axis=0, keepdims=True))
        idx_rows.append(eidx)
    idx_ref[...] = jnp.concatenate(idx_rows, axis=0).T.astype(jnp.int32)
    gate_ref[...] = jnp.concatenate(gate_rows, axis=0).T


def _peer_topk(qp, keys_bf, tm):
    n = qp.shape[0]
    return pl.pallas_call(
        functools.partial(_peer_topk_kernel, tm=tm),
        grid=(n // tm,),
        in_specs=[pl.BlockSpec((tm, PEER_QW), lambda i: (i, 0)),
                  _full((PEER_HEADS * 2, PEER_NKEYS, PEER_KD))],
        out_specs=[pl.BlockSpec((tm, PEER_PICKS), lambda i: (i, 0))] * 2,
        out_shape=[jax.ShapeDtypeStruct((n, PEER_PICKS), jnp.int32),
                   jax.ShapeDtypeStruct((n, PEER_PICKS), F32)],
        compiler_params=_cparams(("parallel",)),
        name="peer_topk",
    )(qp, keys_bf)


GATE_PITCH = PEER_NKEYS + SUBLANES


def _peer_expert_kernel(idx_ref, gate_ref, idx_next, gate_next, hn_ref, h_ref, fn_ref, u_ref, v_ref, out_ref,
                        g_ref, acc_ref, *, tm, te):
    i = pl.program_id(0)
    c = pl.program_id(1)
    tiles = te // PEER_NKEYS
    per_step = tm // (PEER_NKEYS * PEER_NKEYS // te)
    dn = (((1,), (1,)), ((), ()))
    sub = lax.broadcasted_iota(jnp.int32, (PEER_NKEYS, PEER_PICKS), 0)

    def scatter(idx, gates, buf, start, count):
        base = buf * (tm * GATE_PITCH)

        def token(k, carry):
            t = start + k
            e = idx[pl.ds(t, 1), :]
            g = gates[pl.ds(t, 1), :]
            g_hi = g.astype(BF16).astype(F32)
            hit1 = sub == lax.shift_right_logical(e, 7)
            hit2 = sub == (e & (PEER_NKEYS - 1))
            m1 = jnp.where(hit1, 1.0, 0.0).astype(BF16)
            m2_hi = jnp.where(hit2, g_hi, 0.0).astype(BF16)
            m2_lo = jnp.where(hit2, g - g_hi, 0.0).astype(BF16)
            dense = lax.dot_general(jnp.concatenate([m1, m1], axis=1), jnp.concatenate([m2_hi, m2_lo], axis=1), dn,
                                    preferred_element_type=F32)
            g_ref[pl.ds(pl.multiple_of(base + t * GATE_PITCH, SUBLANES), PEER_NKEYS), :] = dense
            return carry

        lax.fori_loop(0, count, token, 0, unroll=8)

    @pl.when((i == 0) & (c == 0))
    def _():
        scatter(idx_ref, gate_ref, 0, 0, tm)

    @pl.when(i + 1 < pl.num_programs(0))
    def _():
        scatter(idx_next, gate_next, lax.rem(i + 1, 2), c * per_step, per_step)

    cur = lax.rem(i, 2) * (tm * GATE_PITCH)
    lin = lax.dot_general(hn_ref[...], u_ref[...], dn, preferred_element_type=F32)
    act = jax.nn.gelu(lin)
    parts = []
    for j in range(tiles):
        gates = g_ref[pl.ds(cur + c * tiles + j, tm, stride=GATE_PITCH), :]
        parts.append((act[:, j * PEER_NKEYS:(j + 1) * PEER_NKEYS] * gates).astype(BF16))
    ffn = jnp.dot(jnp.concatenate(parts, axis=1), v_ref[...], preferred_element_type=F32)

    @pl.when(c == 0)
    def _():
        acc_ref[...] = ffn

    @pl.when(c != 0)
    def _():
        acc_ref[...] += ffn

    @pl.when(c == pl.num_programs(1) - 1)
    def _():
        out_ref[...] = _rms(h_ref[...] + acc_ref[...], fn_ref[...])


def _peer_expert(idx, gate, hn, h, final_norm, u_bf, v_bf, tm, te):
    n = hn.shape[0]
    n_exp = u_bf.shape[0]
    n_tiles = n // tm
    row = lambda w: pl.BlockSpec((tm, w), lambda i, c: (i, 0))
    nxt = pl.BlockSpec((tm, PEER_PICKS), lambda i, c: (jnp.minimum(i + 1, n_tiles - 1), 0))
    tab = pl.BlockSpec((te, D_MODEL), lambda i, c: (c, 0))
    return pl.pallas_call(
        functools.partial(_peer_expert_kernel, tm=tm, te=te),
        grid=(n_tiles, n_exp // te),
        in_specs=[row(PEER_PICKS), row(PEER_PICKS), nxt, nxt, row(D_MODEL), row(D_MODEL),
                  pl.BlockSpec((1, D_MODEL), lambda i, c: (0, 0)), tab, tab],
        out_specs=row(D_MODEL),
        out_shape=jax.ShapeDtypeStruct((n, D_MODEL), F32),
        scratch_shapes=[pltpu.VMEM((2 * tm * GATE_PITCH, PEER_NKEYS), F32), pltpu.VMEM((tm, D_MODEL), F32)],
        compiler_params=_cparams(("arbitrary", "arbitrary")),
        name="peer_expert",
    )(idx, gate, idx, gate, hn, h, final_norm, u_bf, v_bf)


def _row_tile(n, target):
    tm = min(n, target)
    while n % tm:
        tm //= 2
    return tm


def _group(x, pos, attend, wkv0, shift0, wts):
    b, t, _ = x.shape
    n = b * t
    x2 = x.reshape(n, D_MODEL)
    tm = _row_tile(n, 256)
    cos, sin = _rope_tables(jnp.tile(pos, b))
    q, k, v, p_rw = _in_proj(x2, wts["attn_norm"], wts["w_in"], cos, sin, tm)

    o_da = attend(q.reshape(b, t, DA_QK), k.reshape(b, t, DA_QK), v.reshape(b, t, DA_WIDTH))

    rows = _row_tile(t, 256)
    p3 = p_rw.reshape(n // rows, rows, RW_PROJ)
    p_bt = p_rw.reshape(b, t, RW_PROJ)
    prev_rows = p_bt[:, rows - 1::rows][:, :t // rows - 1]
    first = jnp.concatenate([shift0[:, None], prev_rows], axis=1).reshape(n // rows, 1, RW_PROJ)
    gb = _row_tile(n // rows, max(1, 256 // rows))
    r, w, k_mod, v_rw, kk, bb_, g, bonus = _rw_pre(
        p3, first, wts["rw_mu"], wts["rw_wcat"], wts["rw_w0"], wts["rw_a0"],
        wts["rw_k_k"], wts["rw_k_a"], wts["rw_r_k"], wts["seg"], gb)
    slices, per_block = _scan_split(b * RW_HEADS)
    heads = lambda a: a.reshape(b, t, RW_HEADS, RW_HD)
    key = lambda a: _key_cols(heads(a), slices, per_block)
    y_cols, h_cols = _rw_scan(key(r), key(w), key(k_mod), _value_cols(heads(v_rw), slices, per_block),
                              key(kk), key(bb_), _state_cols(wkv0, slices, per_block), tc=_row_tile(t, 64))
    y = _value_cols_inv(y_cols, b, RW_HEADS, slices, per_block)
    wkv = _state_cols_inv(h_cols, b, RW_HEADS, slices, per_block)

    h, hn, qp = _out_proj(x2, o_da.reshape(n, DA_WIDTH), y.reshape(n, RW_WIDTH), bonus, g,
                          wts["rw_ln_w"], wts["rw_ln_b"], wts["seg"], wts["w_out"], wts["ffn_norm"],
                          wts["peer_wq"], tm)
    idx, gate = _peer_topk(qp, wts["peer_keys"], tm)
    out = _peer_expert(idx, gate, hn, h, wts["final_norm"], wts["peer_u"], wts["peer_v"], tm, te=1024)
    return (out.reshape(b, t, D_MODEL), k.reshape(b, t, DA_HEADS, 2 * DA_HD), v.reshape(b, t, DA_HEADS, DA_VD),
            wkv, p_bt[:, -1])


def kernel(x_prompt, x_sample, cache_k, cache_v, state_wkv, state_shift, page_table, attn_norm, w_in, w_out,
           da_lambda_q1, da_lambda_k1, da_lambda_q2, da_lambda_k2, da_subln, rw_mu, rw_w0, rw_w_up, rw_a0,
           rw_a_up, rw_g_up, rw_k_k, rw_k_a, rw_r_k, rw_ln_w, rw_ln_b, ffn_norm, peer_wq, peer_sub_keys,
           peer_u, peer_v, final_norm):
    assert w_in.shape[0] == 1, "single-layer trunk"
    b, t_p, _ = x_prompt.shape
    bd, t_s, _ = x_sample.shape
    past = page_table.shape[1] * PAGE_SIZE
    vec = lambda a: a.reshape(1, -1).astype(F32)

    wcat = jnp.zeros((RW_LORA, 3 * RW_WIDTH), F32)
    wcat = wcat.at[:RW_W_RANK, :RW_WIDTH].set(rw_w_up[0])
    wcat = wcat.at[RW_W_RANK:RW_W_RANK + RW_A_RANK, RW_WIDTH:2 * RW_WIDTH].set(rw_a_up[0])
    wcat = wcat.at[RW_W_RANK + RW_A_RANK:, 2 * RW_WIDTH:].set(rw_g_up[0])
    lane_head = jnp.arange(RW_WIDTH) // RW_HD
    wts = dict(
        attn_norm=vec(attn_norm[0]), w_in=w_in[0].astype(BF16), w_out=w_out[0].astype(BF16),
        rw_mu=vec(rw_mu[0]), rw_wcat=wcat.astype(BF16), rw_w0=vec(rw_w0[0]), rw_a0=vec(rw_a0[0]),
        rw_k_k=vec(rw_k_k[0]), rw_k_a=vec(rw_k_a[0]), rw_r_k=vec(rw_r_k[0]),
        rw_ln_w=vec(rw_ln_w[0]), rw_ln_b=vec(rw_ln_b[0]),
        seg=(lane_head[:, None] == lane_head[None, :]).astype(F32),
        ffn_norm=vec(ffn_norm[0]), peer_wq=peer_wq[0].astype(BF16),
        peer_keys=peer_sub_keys[0].reshape(PEER_HEADS * 2, PEER_NKEYS, PEER_KD).astype(BF16),
        peer_u=peer_u.reshape(-1, D_MODEL).astype(BF16), peer_v=peer_v.reshape(-1, D_MODEL).astype(BF16),
        final_norm=vec(final_norm),
    )
    lams = (vec(da_lambda_q1[0]), vec(da_lambda_k1[0]), vec(da_lambda_q2[0]), vec(da_lambda_k2[0]))
    subln = vec(da_subln[0])

    att_p = lambda q, k, v: _attn_prompt(q, k, v, lams, subln, _row_tile(t_p, 512))
    ck = cache_k.reshape(-1, DA_VD)
    cv = cache_v.reshape(-1, DA_VD)
    att_s = lambda q, k, v: _attn_sample(q, k, v, ck, cv, page_table, lams, subln)

    pos_p = jnp.arange(t_p, dtype=jnp.int32)
    pos_s = past + jnp.arange(t_s, dtype=jnp.int32)
    yp, kp, vp, wp, sp = _group(x_prompt, pos_p, att_p, jnp.zeros((b, RW_HEADS, RW_HD, RW_HD), F32),
                                jnp.zeros((b, RW_PROJ), F32), wts)
    ys, ks, vs, ws, ss = _group(x_sample, pos_s, att_s, state_wkv[0], state_shift[0], wts)
    return (yp, ys, kp[None], vp[None], wp[None], sp[None], ks[None], vs[None], ws[None], ss[None])
```
